```python
import jax, jax.numpy as jnp
from jax import lax
import numpy as np

D_MODEL = 1024
BATCH = 4
SEQ = 8192
DEPTH = 4

N_EVEN = (DEPTH + 1) // 2
N_ODD = DEPTH // 2

MLA_HEADS = 8
MLA_Q_RANK = 384
MLA_KV_RANK = 256
MLA_NOPE = 64
MLA_ROPE = 32
MLA_V = 64
MLA_QK = MLA_NOPE + MLA_ROPE
Q_BLOCK = 128

RET_HEADS = 8
RET_DK = 64
RET_DV = 64
RET_CHUNK = 128
RET_DECAY_BASE = 5.0

GLA_HEADS = 4
GLA_DK = 128
GLA_DV = 256
GLA_GATE_RANK = 16
GLA_TAU = 16.0
GLA_CHUNK = 64

D_FF = 2816
CONV_W = 3

ROPE_THETA = 10000.0
EPS = 1e-6

EVEN_SPLIT = (MLA_Q_RANK, MLA_KV_RANK, MLA_ROPE,
              RET_HEADS * RET_DK, RET_HEADS * RET_DK, RET_HEADS * RET_DV, RET_HEADS * RET_DV)
EVEN_IN = MLA_Q_RANK + MLA_KV_RANK + MLA_ROPE + 2 * RET_HEADS * RET_DK + 2 * RET_HEADS * RET_DV
EVEN_OUT = MLA_HEADS * MLA_V + RET_HEADS * RET_DV
ODD_SPLIT = (GLA_HEADS * GLA_DK, GLA_HEADS * GLA_DK, GLA_HEADS * GLA_DV, GLA_HEADS * GLA_DV,
             GLA_GATE_RANK, GLA_GATE_RANK)
ODD_IN = 2 * GLA_HEADS * GLA_DK + 2 * GLA_HEADS * GLA_DV + 2 * GLA_GATE_RANK
ODD_OUT = GLA_HEADS * GLA_DV

kernel_name = "hybrid_mla_retention_gla_convffn_encoder"


def _split(p, sizes):
    outs, s = [], 0
    for n in sizes:
        outs.append(p[..., s:s + n])
        s += n
    return outs


def _rmsnorm(x, g):
    xf = x.astype(jnp.float32)
    y = xf * lax.rsqrt(jnp.mean(xf * xf, axis=-1, keepdims=True) + EPS)
    return (y * g.astype(jnp.float32)).astype(x.dtype)


def _rope(x, positions):
    half = x.shape[-1] // 2
    inv = ROPE_THETA ** (-jnp.arange(half, dtype=jnp.float32) / half)
    ang = positions.astype(jnp.float32)[:, :, None] * inv
    cos = jnp.cos(ang)[:, :, None, :]
    sin = jnp.sin(ang)[:, :, None, :]
    x1 = x[..., :half].astype(jnp.float32)
    x2 = x[..., half:].astype(jnp.float32)
    return jnp.concatenate([x1 * cos - x2 * sin, x2 * cos + x1 * sin], axis=-1).astype(x.dtype)


def _mla(cq, ckv, k_rope, positions, q_norm, kv_norm, w_uq, w_ukv, q_head_norm, k_head_norm):
    B, S, _ = cq.shape
    H = MLA_HEADS
    q = (_rmsnorm(cq, q_norm) @ w_uq).reshape(B, S, H, MLA_QK)
    kv = (_rmsnorm(ckv, kv_norm) @ w_ukv).reshape(B, S, H, MLA_NOPE + MLA_V)
    k_nope, v = kv[..., :MLA_NOPE], kv[..., MLA_NOPE:]
    k = jnp.concatenate([k_nope, jnp.broadcast_to(k_rope[:, :, None, :], (B, S, H, MLA_ROPE))], axis=-1)
    q = _rmsnorm(q, q_head_norm)
    k = _rmsnorm(k, k_head_norm)
    q = jnp.concatenate([q[..., :MLA_NOPE], _rope(q[..., MLA_NOPE:], positions)], axis=-1)
    k = jnp.concatenate([k[..., :MLA_NOPE], _rope(k[..., MLA_NOPE:], positions)], axis=-1)
    scale = MLA_QK ** -0.5
    nb = S // Q_BLOCK
    qb = q.reshape(B, nb, Q_BLOCK, H, MLA_QK).transpose(1, 0, 2, 3, 4)

    def block(qi):
        s = jnp.einsum('bqhd,bkhd->bhqk', qi, k).astype(jnp.float32) * scale
        p = jax.nn.softmax(s, axis=-1).astype(v.dtype)
        return jnp.einsum('bhqk,bkhe->bqhe', p, v)

    o = lax.map(block, qb)
    return o.transpose(1, 0, 2, 3, 4).reshape(B, S, H * MLA_V)


def _retention_dir(q, k, v, log_gamma, include_diag):
    B, S, H, dk = q.shape
    dv = v.shape[-1]
    C = RET_CHUNK
    n = S // C
    q = q.reshape(B, n, C, H, dk)
    k = k.reshape(B, n, C, H, dk)
    v = v.reshape(B, n, C, H, dv)
    lg = log_gamma.astype(jnp.float32)
    pos = jnp.arange(C, dtype=jnp.float32)
    rel = pos[:, None] - pos[None, :]
    mask = (rel >= 0) if include_diag else (rel > 0)
    decay = jnp.where(mask[None], jnp.exp(lg[:, None, None] * jnp.maximum(rel, 0.0)[None]), 0.0)
    scores = jnp.einsum('bnihd,bnjhd->bnhij', q, k) * decay
    intra = jnp.einsum('bnhij,bnjhe->bnihe', scores, v)
    zeta = jnp.exp(lg[:, None] * (C - 1 - pos)[None])
    chunk_state = jnp.einsum('bnjhd,bnjhe,hj->bnhde', k, v, zeta)
    chunk_decay = jnp.exp(lg * C)[None, :, None, None]

    def step(R, s):
        return chunk_decay * R + s, R

    R0 = jnp.zeros((B, H, dk, dv), chunk_state.dtype)
    _, R_prev = lax.scan(step, R0, jnp.moveaxis(chunk_state, 1, 0))
    R_prev = jnp.moveaxis(R_prev, 0, 1)
    xi = jnp.exp(lg[:, None] * (pos + 1.0)[None]).T
    cross = jnp.einsum('bnihd,bnhde->bnihe', q, R_prev) * xi[None, None, :, :, None]
    return (intra + cross).reshape(B, S, H, dv)


def _retention(rq, rk, rv, rg, positions, theta_fwd, theta_bwd, out_norm):
    B, S, _ = rq.shape
    H = RET_HEADS
    q = _rope(rq.reshape(B, S, H, RET_DK), positions)
    k = _rope(rk.reshape(B, S, H, RET_DK), positions) * (RET_DK ** -0.5)
    v = rv.reshape(B, S, H, RET_DV)
    lg_f = jnp.log1p(-jnp.exp2(-theta_fwd.astype(jnp.float32)))
    lg_b = jnp.log1p(-jnp.exp2(-theta_bwd.astype(jnp.float32)))
    o_f = _retention_dir(q, k, v, lg_f, True)
    o_b = jnp.flip(_retention_dir(jnp.flip(q, 1), jnp.flip(k, 1), jnp.flip(v, 1), lg_b, False), 1)
    o = _rmsnorm(o_f + o_b, out_norm)
    return jax.nn.silu(rg) * o.reshape(B, S, H * RET_DV)


def _gla_dir(q, k, v, log_a, include_diag):
    B, S, H, dk = q.shape
    dv = v.shape[-1]
    C = GLA_CHUNK
    n = S // C
    q = q.reshape(B, n, C, H, dk)
    k = k.reshape(B, n, C, H, dk)
    v = v.reshape(B, n, C, H, dv)
    b = jnp.cumsum(log_a.astype(jnp.float32).reshape(B, n, C, H, dk), axis=2)
    b_mid = b[:, :, C // 2:C // 2 + 1]
    b_last = b[:, :, -1]
    qc = q * jnp.exp(b - b_mid)
    kc = k * jnp.exp(b_mid - b)
    A = jnp.einsum('bnihd,bnjhd->bnhij', qc, kc)
    pos = jnp.arange(C)
    mask = (pos[:, None] >= pos[None, :]) if include_diag else (pos[:, None] > pos[None, :])
    A = jnp.where(mask, A, 0.0)
    intra = jnp.einsum('bnhij,bnjhe->bnihe', A, v)
    k_dec = k * jnp.exp(b_last[:, :, None] - b)
    chunk_state = jnp.einsum('bnjhd,bnjhe->bnhde', k_dec, v)
    chunk_decay = jnp.exp(b_last)

    def step(Sm, xs):
        dcy, st = xs
        return dcy[..., None] * Sm + st, Sm

    S0 = jnp.zeros((B, H, dk, dv), chunk_state.dtype)
    _, S_prev = lax.scan(step, S0, (jnp.moveaxis(chunk_decay, 1, 0), jnp.moveaxis(chunk_state, 1, 0)))
    S_prev = jnp.moveaxis(S_prev, 0, 1)
    inter = jnp.einsum('bnihd,bnhde->bnihe', q * jnp.exp(b), S_prev)
    return (intra + inter).reshape(B, S, H, dv)


def _gla(gq, gk, gv, gr, ga_f, ga_b, w_gate_fwd, b_gate_fwd, w_gate_bwd, b_gate_bwd, out_norm):
    B, S, _ = gq.shape
    H = GLA_HEADS
    q = gq.reshape(B, S, H, GLA_DK) * (GLA_DK ** -0.5)
    k = gk.reshape(B, S, H, GLA_DK)
    v = gv.reshape(B, S, H, GLA_DV)
    la_f = (jax.nn.log_sigmoid((ga_f @ w_gate_fwd + b_gate_fwd).astype(jnp.float32)) / GLA_TAU).reshape(B, S, H, GLA_DK)
    la_b = (jax.nn.log_sigmoid((ga_b @ w_gate_bwd + b_gate_bwd).astype(jnp.float32)) / GLA_TAU).reshape(B, S, H, GLA_DK)
    o_f = _gla_dir(q, k, v, la_f, True)
    o_b = jnp.flip(_gla_dir(jnp.flip(q, 1), jnp.flip(k, 1), jnp.flip(v, 1), jnp.flip(la_b, 1), False), 1)
    o = _rmsnorm(o_f + o_b, out_norm)
    return jax.nn.silu(gr) * o.reshape(B, S, H * GLA_DV)


def _conv_ffn(x, norm_g, w_up, conv_w, conv_b, w_down):
    h = _rmsnorm(x, norm_g)
    up = h @ w_up
    gate, val = up[..., :D_FF], up[..., D_FF:]
    gate = lax.conv_general_dilated(gate, conv_w[:, None, :].astype(gate.dtype), window_strides=(1,),
                                    padding='SAME', dimension_numbers=('NWC', 'WIO', 'NWC'),
                                    feature_group_count=D_FF) + conv_b
    return (jax.nn.silu(gate) * val) @ w_down


def setup_inputs(seed: int = 0) -> dict:
    key = jax.random.key(seed)
    ks = iter(jax.random.split(key, 40))

    def dense(shape, fan_in):
        return jax.random.normal(next(ks), shape, jnp.float32) * (fan_in ** -0.5)

    def gain(shape):
        return 1.0 + 0.02 * jax.random.normal(next(ks), shape, jnp.float32)

    def small(shape, s):
        return s * jax.random.normal(next(ks), shape, jnp.float32)

    x = jax.random.normal(next(ks), (BATCH, SEQ, D_MODEL), jnp.float32)
    start = jax.random.randint(next(ks), (BATCH, 1), 0, 4096)
    positions = (start + jnp.arange(SEQ)[None, :]).astype(jnp.int32)
    ret_base = RET_DECAY_BASE + jnp.arange(RET_HEADS, dtype=jnp.float32)
    return {
        "x": x,
        "positions": positions,
        "mix_norm_even": gain((N_EVEN, D_MODEL)),
        "w_in_even": dense((N_EVEN, D_MODEL, EVEN_IN), D_MODEL),
        "mla_q_norm": gain((N_EVEN, MLA_Q_RANK)),
        "mla_kv_norm": gain((N_EVEN, MLA_KV_RANK)),
        "mla_w_uq": dense((N_EVEN, MLA_Q_RANK, MLA_HEADS * MLA_QK), MLA_Q_RANK),
        "mla_w_ukv": dense((N_EVEN, MLA_KV_RANK, MLA_HEADS * (MLA_NOPE + MLA_V)), MLA_KV_RANK),
        "mla_q_head_norm": gain((N_EVEN, MLA_QK)),
        "mla_k_head_norm": gain((N_EVEN, MLA_QK)),
        "ret_theta_fwd": ret_base + small((N_EVEN, RET_HEADS), 0.1),
        "ret_theta_bwd": ret_base + small((N_EVEN, RET_HEADS), 0.1),
        "ret_out_norm": gain((N_EVEN, RET_HEADS, RET_DV)),
        "w_out_even": dense((N_EVEN, EVEN_OUT, D_MODEL), EVEN_OUT),
        "mix_norm_odd": gain((N_ODD, D_MODEL)),
        "w_in_odd": dense((N_ODD, D_MODEL, ODD_IN), D_MODEL),
        "gla_w_gate_fwd": dense((N_ODD, GLA_GATE_RANK, GLA_HEADS * GLA_DK), GLA_GATE_RANK),
        "gla_b_gate_fwd": small((N_ODD, GLA_HEADS * GLA_DK), 0.1),
        "gla_w_gate_bwd": dense((N_ODD, GLA_GATE_RANK, GLA_HEADS * GLA_DK), GLA_GATE_RANK),
        "gla_b_gate_bwd": small((N_ODD, GLA_HEADS * GLA_DK), 0.1),
        "gla_out_norm": gain((N_ODD, GLA_HEADS, GLA_DV)),
        "w_out_odd": dense((N_ODD, ODD_OUT, D_MODEL), ODD_OUT),
        "ffn_norm": gain((DEPTH, D_MODEL)),
        "ffn_w_up": dense((DEPTH, D_MODEL, 2 * D_FF), D_MODEL),
        "ffn_conv_w": dense((DEPTH, CONV_W, D_FF), CONV_W),
        "ffn_conv_b": small((DEPTH, D_FF), 0.02),
        "ffn_w_down": dense((DEPTH, D_FF, D_MODEL), D_FF),
    }


def reference(x, positions, mix_norm_even, w_in_even, mla_q_norm, mla_kv_norm, mla_w_uq, mla_w_ukv,
              mla_q_head_norm, mla_k_head_norm, ret_theta_fwd, ret_theta_bwd, ret_out_norm, w_out_even,
              mix_norm_odd, w_in_odd, gla_w_gate_fwd, gla_b_gate_fwd, gla_w_gate_bwd, gla_b_gate_bwd,
              gla_out_norm, w_out_odd, ffn_norm, ffn_w_up, ffn_conv_w, ffn_conv_b, ffn_w_down):
    for layer in range(DEPTH):
        i = layer // 2
        if layer % 2 == 0:
            h = _rmsnorm(x, mix_norm_even[i])
            cq, ckv, k_rope, rq, rk, rv, rg = _split(h @ w_in_even[i], EVEN_SPLIT)
            a = _mla(cq, ckv, k_rope, positions, mla_q_norm[i], mla_kv_norm[i], mla_w_uq[i], mla_w_ukv[i],
                     mla_q_head_norm[i], mla_k_head_norm[i])
            r = _retention(rq, rk, rv, rg, positions, ret_theta_fwd[i], ret_theta_bwd[i], ret_out_norm[i])
            x = x + jnp.concatenate([a, r], axis=-1) @ w_out_even[i]
        else:
            h = _rmsnorm(x, mix_norm_odd[i])
            gq, gk, gv, gr, ga_f, ga_b = _split(h @ w_in_odd[i], ODD_SPLIT)
            g = _gla(gq, gk, gv, gr, ga_f, ga_b, gla_w_gate_fwd[i], gla_b_gate_fwd[i],
                     gla_w_gate_bwd[i], gla_b_gate_bwd[i], gla_out_norm[i])
            x = x + g @ w_out_odd[i]
        x = x + _conv_ffn(x, ffn_norm[layer], ffn_w_up[layer], ffn_conv_w[layer], ffn_conv_b[layer],
                          ffn_w_down[layer])
    return x
```

```python
import functools

import numpy as np
import jax
import jax.numpy as jnp
from jax import lax
from jax.experimental import pallas as pl
from jax.experimental.pallas import tpu as pltpu

F32 = jnp.float32
BF16 = jnp.bfloat16

D_MODEL = 1024
BATCH = 4
SEQ = 8192
DEPTH = 4
TOKENS = BATCH * SEQ

MLA_HEADS = 8
MLA_Q_RANK = 384
MLA_KV_RANK = 256
MLA_NOPE = 64
MLA_ROPE = 32
MLA_V = 64
MLA_QK = MLA_NOPE + MLA_ROPE

RET_HEADS = 8
RET_DK = 64
RET_DV = 64
RET_CHUNK = 128

GLA_HEADS = 4
GLA_DK = 128
GLA_DV = 256
GLA_GATE_RANK = 16
GLA_TAU = 16.0
GLA_CHUNK = 64

D_FF = 2816
ROPE_THETA = 10000.0
EPS = 1e-6

LANES = 128
SUBLANES = 8
VMEM_LIMIT = 56 * 2**20

TM = 512
T_ATT_Q = 256
T_ATT_K = 512
T_SCAN = 512


def _params(*sem):
    return pltpu.CompilerParams(dimension_semantics=sem, vmem_limit_bytes=VMEM_LIMIT)


def _rms(x, g):
    return x * lax.rsqrt(jnp.mean(x * x, axis=-1, keepdims=True) + EPS) * g


def _dot(a, b):
    return jnp.dot(a, b, preferred_element_type=F32)


def _dot_nt(a, b):
    return lax.dot_general(a, b, (((1,), (1,)), ((), ())), preferred_element_type=F32)


def _dot_tn(a, b):
    return lax.dot_general(a, b, (((0,), (0,)), ((), ())), preferred_element_type=F32)


def _silu(x):
    return x * jax.nn.sigmoid(x)


def _rope_tables_kernel(pos_ref, inv_ref, sgn_ref, cos_ref, sin_ref):
    ang = pos_ref[...].astype(F32) * inv_ref[...]
    cos_ref[...] = jnp.cos(ang)
    sin_ref[...] = jnp.sin(ang) * sgn_ref[...]


def _rope_tables(pos_col, inv_lane, sgn_lane):
    return pl.pallas_call(
        _rope_tables_kernel,
        grid=(TOKENS // TM,),
        in_specs=[pl.BlockSpec((TM, 1), lambda i: (i, 0)),
                  pl.BlockSpec((1, LANES), lambda i: (0, 0)),
                  pl.BlockSpec((1, LANES), lambda i: (0, 0))],
        out_specs=[pl.BlockSpec((TM, LANES), lambda i: (i, 0))] * 2,
        out_shape=[jax.ShapeDtypeStruct((TOKENS, LANES), F32)] * 2,
        compiler_params=_params("parallel"),
        name="rope_tables",
    )(pos_col, inv_lane, sgn_lane)


def _rope_lane_constants():
    lane = np.arange(LANES)
    half = RET_DK // 2
    inv = ROPE_THETA ** (-np.arange(half, dtype=np.float32) / half)
    ret_inv = inv[lane % half].astype(np.float32)
    ret_sgn = np.where(lane % RET_DK < half, -1.0, 1.0).astype(np.float32)
    half = MLA_ROPE // 2
    inv = ROPE_THETA ** (-np.arange(half, dtype=np.float32) / half)
    in_rope = (lane >= MLA_NOPE) & (lane < MLA_QK)
    mla_inv = np.where(in_rope, inv[(lane - MLA_NOPE) % half], 0.0).astype(np.float32)
    mla_sgn = np.where(in_rope, np.where(lane < MLA_NOPE + half, -1.0, 1.0), 0.0).astype(np.float32)
    f = lambda a: jnp.asarray(a).reshape(1, LANES)
    return f(ret_inv), f(ret_sgn), f(mla_inv), f(mla_sgn)


def _norm_proj_kernel(x_ref, g_ref, w_ref, *out_refs, splits, chunk):
    h = _rms(x_ref[...], g_ref[...]).astype(BF16)
    for o_ref, (a, b) in zip(out_refs, splits):
        for c in range(a, b, chunk):
            e = min(c + chunk, b)
            o_ref[:, c - a:e - a] = _dot(h, w_ref[:, c:e]).astype(o_ref.dtype)


def _norm_proj(x, g, w, splits, name):
    n = w.shape[1]
    return pl.pallas_call(
        functools.partial(_norm_proj_kernel, splits=splits, chunk=512),
        grid=(TOKENS // TM,),
        in_specs=[pl.BlockSpec((TM, D_MODEL), lambda i: (i, 0)),
                  pl.BlockSpec((1, D_MODEL), lambda i: (0, 0)),
                  pl.BlockSpec((D_MODEL, n), lambda i: (0, 0))],
        out_specs=[pl.BlockSpec((TM, b - a), lambda i: (i, 0)) for a, b in splits],
        out_shape=[jax.ShapeDtypeStruct((TOKENS, b - a), BF16) for a, b in splits],
        compiler_params=_params("parallel"),
        name=name,
    )(x, g.reshape(1, D_MODEL), w)


def _out_proj_kernel(*refs, n_in):
    x_ref = refs[0]
    acts = refs[1:1 + n_in]
    ws = refs[1 + n_in:1 + 2 * n_in]
    o_ref = refs[1 + 2 * n_in]
    acc = x_ref[...]
    for a_ref, w_ref in zip(acts, ws):
        acc = acc + _dot(a_ref[...], w_ref[...])
    o_ref[...] = acc


def _out_proj(x, acts, ws, name):
    n_in = len(acts)
    return pl.pallas_call(
        functools.partial(_out_proj_kernel, n_in=n_in),
        grid=(TOKENS // TM,),
        in_specs=([pl.BlockSpec((TM, D_MODEL), lambda i: (i, 0))]
                  + [pl.BlockSpec((TM, a.shape[1]), lambda i: (i, 0)) for a in acts]
                  + [pl.BlockSpec(w.shape, lambda i: (0, 0)) for w in ws]),
        out_specs=pl.BlockSpec((TM, D_MODEL), lambda i: (i, 0)),
        out_shape=jax.ShapeDtypeStruct((TOKENS, D_MODEL), F32),
        compiler_params=_params("parallel"),
        name=name,
    )(x, *acts, *ws)


FF_CHUNK = 256


def _ffn_down_kernel(gate_ref, gprev_ref, gnext_ref, val_ref, cw_ref, cb_ref, wd_ref, x_ref, o_ref,
                     *, tiles_per_seq):
    j = pl.program_id(0) % tiles_per_seq
    first = j == 0
    last = j == tiles_per_seq - 1
    row = lax.broadcasted_iota(jnp.int32, (TM, 1), 0)
    acc = x_ref[...]
    for c in range(0, D_FF, FF_CHUNK):
        sl = slice(c, c + FF_CHUNK)
        g = gate_ref[:, sl].astype(F32)
        prev_row = jnp.where(first, 0.0, gprev_ref[SUBLANES - 1:SUBLANES, sl].astype(F32))
        next_row = jnp.where(last, 0.0, gnext_ref[0:1, sl].astype(F32))
        g_m1 = jnp.where(row == 0, prev_row, pltpu.roll(g, 1, 0))
        g_p1 = jnp.where(row == TM - 1, next_row, pltpu.roll(g, TM - 1, 0))
        conv = cw_ref[0:1, sl] * g_m1 + cw_ref[1:2, sl] * g + cw_ref[2:3, sl] * g_p1 + cb_ref[:, sl]
        act = (_silu(conv) * val_ref[:, sl].astype(F32)).astype(BF16)
        acc = acc + _dot(act, wd_ref[sl, :])
    o_ref[...] = acc


def _ffn_down(gate, val, conv_w, conv_b, w_down, x):
    tiles_per_seq = SEQ // TM
    rows8 = TM // SUBLANES
    n8 = TOKENS // SUBLANES
    return pl.pallas_call(
        functools.partial(_ffn_down_kernel, tiles_per_seq=tiles_per_seq),
        grid=(TOKENS // TM,),
        in_specs=[pl.BlockSpec((TM, D_FF), lambda i: (i, 0)),
                  pl.BlockSpec((SUBLANES, D_FF), lambda i: (jnp.maximum(i * rows8 - 1, 0), 0)),
                  pl.BlockSpec((SUBLANES, D_FF), lambda i: (jnp.minimum((i + 1) * rows8, n8 - 1), 0)),
                  pl.BlockSpec((TM, D_FF), lambda i: (i, 0)),
                  pl.BlockSpec((3, D_FF), lambda i: (0, 0)),
                  pl.BlockSpec((1, D_FF), lambda i: (0, 0)),
                  pl.BlockSpec((D_FF, D_MODEL), lambda i: (0, 0)),
                  pl.BlockSpec((TM, D_MODEL), lambda i: (i, 0))],
        out_specs=pl.BlockSpec((TM, D_MODEL), lambda i: (i, 0)),
        out_shape=jax.ShapeDtypeStruct((TOKENS, D_MODEL), F32),
        compiler_params=_params("parallel"),
        name="ffn_down",
    )(gate, gate, gate, val, conv_w, conv_b.reshape(1, D_FF), w_down, x)


def _conv_ffn(x, norm_g, w_up, conv_w, conv_b, w_down):
    gate, val = _norm_proj(x, norm_g, w_up, ((0, D_FF), (D_FF, 2 * D_FF)), "ffn_up")
    return _ffn_down(gate, val, conv_w, conv_b, w_down, x)


def _mla_prep_kernel(cq_ref, ckv_ref, kr_ref, cos_ref, sin_ref, qn_ref, kvn_ref, wuq_ref, wk_ref, wv_ref,
                     e_ref, gq_ref, gk_ref, q_out, k_out, v_out):
    cos = cos_ref[...]
    sin = sin_ref[...]
    lane = lax.broadcasted_iota(jnp.int32, (1, LANES), 1)
    take_upper = lane < MLA_NOPE + MLA_ROPE // 2
    ones_col = (lane == MLA_V).astype(F32)
    cqn = _rms(cq_ref[...].astype(F32), qn_ref[...]).astype(BF16)
    ckvn = _rms(ckv_ref[...].astype(F32), kvn_ref[...]).astype(BF16)
    qf = _dot(cqn, wuq_ref[...])
    kf = _dot(ckvn, wk_ref[...]) + _dot(kr_ref[...], e_ref[...])
    vf = _dot(ckvn, wv_ref[...])
    scale = MLA_QK ** -0.5

    def head_norm_rope(xh, g):
        xh = xh * lax.rsqrt(jnp.sum(xh * xh, axis=-1, keepdims=True) * (1.0 / MLA_QK) + EPS) * g
        half = MLA_ROPE // 2
        rolled = jnp.where(take_upper, pltpu.roll(xh, LANES - half, 1), pltpu.roll(xh, half, 1))
        return xh * cos + rolled * sin

    for h in range(MLA_HEADS):
        sl = slice(h * LANES, (h + 1) * LANES)
        q_out[0, h] = (head_norm_rope(qf[:, sl], gq_ref[...]) * scale).astype(BF16)
        k_out[0, h] = head_norm_rope(kf[:, sl], gk_ref[...]).astype(BF16)
        v_out[0, h] = (vf[:, sl] + ones_col).astype(BF16)


def _mla_prep(cq, ckv, kr, cos, sin, q_norm, kv_norm, wuq_pad, wk_pad, wv_pad, e_mat, gq_pad, gk_pad):
    tps = SEQ // TM
    tok = lambda n: pl.BlockSpec((TM, n), lambda i: (i, 0))
    full = lambda a: pl.BlockSpec(a.shape, lambda i: (0,) * a.ndim)
    head_major = pl.BlockSpec((1, MLA_HEADS, TM, LANES), lambda i: (i // tps, 0, i % tps, 0))
    args = (cq, ckv, kr, cos, sin, q_norm, kv_norm, wuq_pad, wk_pad, wv_pad, e_mat, gq_pad, gk_pad)
    return pl.pallas_call(
        _mla_prep_kernel,
        grid=(TOKENS // TM,),
        in_specs=[tok(MLA_Q_RANK), tok(MLA_KV_RANK), tok(LANES), tok(LANES), tok(LANES)]
                 + [full(a) for a in args[5:]],
        out_specs=[head_major] * 3,
        out_shape=[jax.ShapeDtypeStruct((BATCH, MLA_HEADS, SEQ, LANES), BF16)] * 3,
        compiler_params=_params("parallel"),
        name="mla_prep",
    )(*args)


def _attn_kernel(q_ref, k_ref, v_ref, o_ref):
    lane = lax.broadcasted_iota(jnp.int32, (1, LANES), 1)
    outs = []
    for hh in range(2):
        q = q_ref[0, hh]

        def body(kb, carry):
            m, acc = carry
            off = pl.multiple_of(kb * T_ATT_K, T_ATT_K)
            k = k_ref[0, hh, pl.ds(off, T_ATT_K), :]
            v = v_ref[0, hh, pl.ds(off, T_ATT_K), :]
            s = _dot_nt(q, k)
            m_new = jnp.maximum(m, jnp.max(s, axis=1, keepdims=True))
            alpha = jnp.exp(m - m_new)
            p = jnp.exp(s - m_new)
            acc = alpha * acc + _dot(p.astype(BF16), v)
            return m_new, acc

        m0 = jnp.full((T_ATT_Q, 1), -jnp.inf, F32)
        acc0 = jnp.zeros((T_ATT_Q, LANES), F32)
        _, acc = lax.fori_loop(0, SEQ // T_ATT_K, body, (m0, acc0))
        outs.append(acc / acc[:, MLA_V:MLA_V + 1])
    o = jnp.where(lane < MLA_V, outs[0], pltpu.roll(outs[1], MLA_V, 1))
    o_ref[0] = o.astype(BF16)


def _attention(q, k, v):
    return pl.pallas_call(
        _attn_kernel,
        grid=(BATCH, MLA_HEADS // 2, SEQ // T_ATT_Q),
        in_specs=[pl.BlockSpec((1, 2, T_ATT_Q, LANES), lambda b, h, i: (b, h, i, 0)),
                  pl.BlockSpec((1, 2, SEQ, LANES), lambda b, h, i: (b, h, 0, 0)),
                  pl.BlockSpec((1, 2, SEQ, LANES), lambda b, h, i: (b, h, 0, 0))],
        out_specs=pl.BlockSpec((1, T_ATT_Q, LANES), lambda b, h, i: (b, i, h)),
        out_shape=jax.ShapeDtypeStruct((BATCH, SEQ, MLA_HEADS * MLA_V), BF16),
        compiler_params=_params("parallel", "parallel", "parallel"),
        name="mla_attention",
    )(q, k, v)


def _ret_rope(x, cos, sin, lower_half):
    half = RET_DK // 2
    rolled = jnp.where(lower_half, pltpu.roll(x, LANES - half, 1), pltpu.roll(x, half, 1))
    return x * cos + rolled * sin


def _ret_common():
    lane = lax.broadcasted_iota(jnp.int32, (1, LANES), 1)
    rowi = lax.broadcasted_iota(jnp.int32, (RET_CHUNK, 1), 0)
    head0 = lane < RET_DK
    same_head = (rowi < RET_DK) == head0
    lower_half = (lane % RET_DK) < RET_DK // 2
    return lane, rowi, head0, same_head, lower_half


def _ret_fwd_kernel(q_ref, k_ref, v_ref, cos_ref, sin_ref, lgf_ref, lgb_ref, o_ref, state_ref):
    @pl.when(pl.program_id(2) == 0)
    def _():
        state_ref[...] = jnp.zeros_like(state_ref)

    lane, rowi, head0, same_head, lower_half = _ret_common()
    C = RET_CHUNK
    row = rowi.astype(F32)
    col = lane.astype(F32)
    lgf = lgf_ref[0]
    lgb = lgb_ref[0]
    rel = row - col

    def decay(lf, lb):
        return jnp.exp(jnp.where(rel >= 0, lf * rel, -lb * rel))

    d0 = decay(lgf[:, 0:1], lgb[:, 0:1])
    d1 = decay(lgf[:, RET_DK:RET_DK + 1], lgb[:, RET_DK:RET_DK + 1])
    xi = jnp.exp(lgf * (row + 1.0))
    zeta = jnp.exp(lgf * (C - 1.0 - row))
    chunk_decay = jnp.exp(lgf * C)
    for c in range(T_SCAN // C):
        sl = pl.ds(c * C, C)
        cos = cos_ref[sl, :]
        sin = sin_ref[sl, :]
        q = _ret_rope(q_ref[sl, :].astype(F32), cos, sin, lower_half)
        k = _ret_rope(k_ref[sl, :].astype(F32), cos, sin, lower_half) * (RET_DK ** -0.5)
        v = v_ref[sl, :]
        kb = k.astype(BF16)
        s0 = _dot_nt(jnp.where(head0, q, 0.0).astype(BF16), kb) * d0
        s1 = _dot_nt(jnp.where(head0, 0.0, q).astype(BF16), kb) * d1
        intra = jnp.where(head0, _dot(s0.astype(BF16), v), _dot(s1.astype(BF16), v))
        state = state_ref[...]
        cross = _dot(q.astype(BF16), state.astype(BF16)) * xi
        o_ref[sl, :] = intra + cross
        upd = _dot_tn((k * zeta).astype(BF16), v)
        state_ref[...] = state * chunk_decay + jnp.where(same_head, upd, 0.0)


def _ret_bwd_kernel(q_ref, k_ref, v_ref, g_ref, o1_ref, cos_ref, sin_ref, lgb_ref, gain_ref, o_ref, state_ref):
    @pl.when(pl.program_id(2) == 0)
    def _():
        state_ref[...] = jnp.zeros_like(state_ref)

    lane, rowi, head0, same_head, lower_half = _ret_common()
    C = RET_CHUNK
    row = rowi.astype(F32)
    lgb = lgb_ref[0]
    xi = jnp.exp(lgb * (C - row))
    zeta = jnp.exp(lgb * row)
    chunk_decay = jnp.exp(lgb * C)
    for c in reversed(range(T_SCAN // C)):
        sl = pl.ds(c * C, C)
        cos = cos_ref[sl, :]
        sin = sin_ref[sl, :]
        q = _ret_rope(q_ref[sl, :].astype(F32), cos, sin, lower_half)
        k = _ret_rope(k_ref[sl, :].astype(F32), cos, sin, lower_half) * (RET_DK ** -0.5)
        v = v_ref[sl, :]
        state = state_ref[...]
        o = o1_ref[sl, :] + _dot(q.astype(BF16), state.astype(BF16)) * xi
        o2 = o * o
        ss0 = jnp.sum(jnp.where(head0, o2, 0.0), axis=-1, keepdims=True)
        ss1 = jnp.sum(jnp.where(head0, 0.0, o2), axis=-1, keepdims=True)
        ms = jnp.where(head0, ss0, ss1) * (1.0 / RET_DV)
        y = o * lax.rsqrt(ms + EPS) * gain_ref[...]
        o_ref[sl, :] = (_silu(g_ref[sl, :].astype(F32)) * y).astype(BF16)
        upd = _dot_tn((k * zeta).astype(BF16), v)
        state_ref[...] = state * chunk_decay + jnp.where(same_head, upd, 0.0)


def _retention(rq, rk, rv, rg, cos, sin, lgf_lane, lgb_lane, gain):
    nt = SEQ // T_SCAN
    pairs = RET_HEADS // 2
    fwd_tok = lambda b, p, t: (b * nt + t, p)
    fwd_tab = lambda b, p, t: (b * nt + t, 0)
    bwd_tok = lambda b, p, t: (b * nt + nt - 1 - t, p)
    bwd_tab = lambda b, p, t: (b * nt + nt - 1 - t, 0)
    per_pair = lambda b, p, t: (p, 0, 0)
    blk = (T_SCAN, LANES)
    o1 = pl.pallas_call(
        _ret_fwd_kernel,
        grid=(BATCH, pairs, nt),
        in_specs=[pl.BlockSpec(blk, fwd_tok)] * 3 + [pl.BlockSpec(blk, fwd_tab)] * 2
                 + [pl.BlockSpec((1, 1, LANES), per_pair)] * 2,
        out_specs=pl.BlockSpec(blk, fwd_tok),
        out_shape=jax.ShapeDtypeStruct((TOKENS, RET_HEADS * RET_DV), F32),
        scratch_shapes=[pltpu.VMEM((LANES, LANES), F32)],
        compiler_params=_params("parallel", "parallel", "arbitrary"),
        name="retention_fwd",
    )(rq, rk, rv, cos, sin, lgf_lane, lgb_lane)
    return pl.pallas_call(
        _ret_bwd_kernel,
        grid=(BATCH, pairs, nt),
        in_specs=[pl.BlockSpec(blk, bwd_tok)] * 5 + [pl.BlockSpec(blk, bwd_tab)] * 2
                 + [pl.BlockSpec((1, 1, LANES), per_pair), pl.BlockSpec((1, LANES), lambda b, p, t: (0, p))],
        out_specs=pl.BlockSpec(blk, bwd_tok),
        out_shape=jax.ShapeDtypeStruct((TOKENS, RET_HEADS * RET_DV), BF16),
        scratch_shapes=[pltpu.VMEM((LANES, LANES), F32)],
        compiler_params=_params("parallel", "parallel", "arbitrary"),
        name="retention_bwd",
    )(rq, rk, rv, rg, o1, cos, sin, lgb_lane, gain)


def _log_sigmoid(z):
    return jnp.minimum(z, 0.0) - jnp.log(1.0 + jnp.exp(-jnp.abs(z)))


def _chunk_cumsum(tri, la):
    hi = la.astype(BF16)
    lo = (la - hi.astype(F32)).astype(BF16)
    return _dot(tri, hi) + _dot(tri, lo)


def _gla_fwd_kernel(q_ref, k_ref, v_ref, ga_ref, wg_ref, bg_ref, o_ref, state_ref):
    @pl.when(pl.program_id(2) == 0)
    def _():
        state_ref[...] = jnp.zeros_like(state_ref)

    C = GLA_CHUNK
    ri = lax.broadcasted_iota(jnp.int32, (C, C), 0)
    ci = lax.broadcasted_iota(jnp.int32, (C, C), 1)
    lower = ri >= ci
    tri = lower.astype(BF16)
    la_all = _log_sigmoid(_dot(ga_ref[...], wg_ref[...]) + bg_ref[...]) * (1.0 / GLA_TAU)
    for c in range(T_SCAN // C):
        sl = pl.ds(c * C, C)
        b = _chunk_cumsum(tri, la_all[c * C:(c + 1) * C])
        b_mid = b[C // 2:C // 2 + 1]
        b_last = b[C - 1:C]
        q = q_ref[sl, :].astype(F32) * (GLA_DK ** -0.5)
        k = k_ref[sl, :].astype(F32)
        v = v_ref[sl, :]
        a = _dot_nt((q * jnp.exp(b - b_mid)).astype(BF16), (k * jnp.exp(b_mid - b)).astype(BF16))
        a = jnp.where(lower, a, 0.0)
        state = state_ref[...]
        inter = _dot_nt((q * jnp.exp(b)).astype(BF16), state.astype(BF16))
        o_ref[sl, :] = _dot(a.astype(BF16), v) + inter
        k_dec = (k * jnp.exp(b_last - b)).astype(BF16)
        state_ref[...] = state * jnp.exp(b_last) + _dot_tn(v, k_dec)


def _gla_bwd_kernel(q_ref, k_ref, v_ref, r_ref, o1_ref, ga_ref, wg_ref, bg_ref, gain_ref, o_ref, state_ref):
    @pl.when(pl.program_id(2) == 0)
    def _():
        state_ref[...] = jnp.zeros_like(state_ref)

    C = GLA_CHUNK
    ri = lax.broadcasted_iota(jnp.int32, (C, C), 0)
    ci = lax.broadcasted_iota(jnp.int32, (C, C), 1)
    upper_incl = (ci >= ri).astype(BF16)
    upper_strict = ci > ri
    la_all = _log_sigmoid(_dot(ga_ref[...], wg_ref[...]) + bg_ref[...]) * (1.0 / GLA_TAU)
    for c in reversed(range(T_SCAN // C)):
        sl = pl.ds(c * C, C)
        b = _chunk_cumsum(upper_incl, la_all[c * C:(c + 1) * C])
        b_mid = b[C // 2 - 1:C // 2]
        b_last = b[0:1]
        q = q_ref[sl, :].astype(F32) * (GLA_DK ** -0.5)
        k = k_ref[sl, :].astype(F32)
        v = v_ref[sl, :]
        a = _dot_nt((q * jnp.exp(b - b_mid)).astype(BF16), (k * jnp.exp(b_mid - b)).astype(BF16))
        a = jnp.where(upper_strict, a, 0.0)
        state = state_ref[...]
        inter = _dot_nt((q * jnp.exp(b)).astype(BF16), state.astype(BF16))
        o = o1_ref[sl, :] + _dot(a.astype(BF16), v) + inter
        y = _rms(o, gain_ref[...])
        o_ref[sl, :] = (_silu(r_ref[sl, :].astype(F32)) * y).astype(BF16)
        k_dec = (k * jnp.exp(b_last - b)).astype(BF16)
        state_ref[...] = state * jnp.exp(b_last) + _dot_tn(v, k_dec)


def _gla(gq, gk, gv, gr, ga, wgf_pad, bgf, wgb_pad, bgb, gain):
    nt = SEQ // T_SCAN
    fwd_tok = lambda b, h, t: (b * nt + t, h)
    fwd_tab = lambda b, h, t: (b * nt + t, 0)
    bwd_tok = lambda b, h, t: (b * nt + nt - 1 - t, h)
    bwd_tab = lambda b, h, t: (b * nt + nt - 1 - t, 0)
    per_head = lambda b, h, t: (0, h)
    kblk = (T_SCAN, GLA_DK)
    vblk = (T_SCAN, GLA_DV)
    o1 = pl.pallas_call(
        _gla_fwd_kernel,
        grid=(BATCH, GLA_HEADS, nt),
        in_specs=[pl.BlockSpec(kblk, fwd_tok), pl.BlockSpec(kblk, fwd_tok), pl.BlockSpec(vblk, fwd_tok),
                  pl.BlockSpec((T_SCAN, LANES), fwd_tab),
                  pl.BlockSpec((LANES, GLA_DK), per_head), pl.BlockSpec((1, GLA_DK), per_head)],
        out_specs=pl.BlockSpec(vblk, fwd_tok),
        out_shape=jax.ShapeDtypeStruct((TOKENS, GLA_HEADS * GLA_DV), F32),
        scratch_shapes=[pltpu.VMEM((GLA_DV, GLA_DK), F32)],
        compiler_params=_params("parallel", "parallel", "arbitrary"),
        name="gla_fwd",
    )(gq, gk, gv, ga, wgf_pad, bgf)
    return pl.pallas_call(
        _gla_bwd_kernel,
        grid=(BATCH, GLA_HEADS, nt),
        in_specs=[pl.BlockSpec(kblk, bwd_tok), pl.BlockSpec(kblk, bwd_tok), pl.BlockSpec(vblk, bwd_tok),
                  pl.BlockSpec(vblk, bwd_tok), pl.BlockSpec(vblk, bwd_tok),
                  pl.BlockSpec((T_SCAN, LANES), bwd_tab),
                  pl.BlockSpec((LANES, GLA_DK), per_head), pl.BlockSpec((1, GLA_DK), per_head),
                  pl.BlockSpec((1, GLA_DV), per_head)],
        out_specs=pl.BlockSpec(vblk, bwd_tok),
        out_shape=jax.ShapeDtypeStruct((TOKENS, GLA_HEADS * GLA_DV), BF16),
        scratch_shapes=[pltpu.VMEM((GLA_DV, GLA_DK), F32)],
        compiler_params=_params("parallel", "parallel", "arbitrary"),
        name="gla_bwd",
    )(gq, gk, gv, gr, o1, ga, wgb_pad, bgb, gain)


def _even_weights(w_in, w_uq, w_ukv, q_head_norm, k_head_norm):
    kr0 = MLA_Q_RANK + MLA_KV_RANK
    w_in_r = jnp.concatenate(
        [w_in[:, :kr0], w_in[:, kr0 + MLA_ROPE:], w_in[:, kr0:kr0 + MLA_ROPE],
         jnp.zeros((D_MODEL, LANES - MLA_ROPE), w_in.dtype)], axis=1).astype(BF16)
    pad_q = jnp.zeros((MLA_Q_RANK, MLA_HEADS, LANES - MLA_QK), F32)
    wuq_pad = jnp.concatenate([w_uq.reshape(MLA_Q_RANK, MLA_HEADS, MLA_QK), pad_q], axis=2)
    wuq_pad = wuq_pad.reshape(MLA_Q_RANK, MLA_HEADS * LANES).astype(BF16)
    wkv = w_ukv.reshape(MLA_KV_RANK, MLA_HEADS, MLA_NOPE + MLA_V)
    pad_kv = jnp.zeros((MLA_KV_RANK, MLA_HEADS, LANES - MLA_NOPE), F32)
    wk_pad = jnp.concatenate([wkv[:, :, :MLA_NOPE], pad_kv], axis=2).reshape(MLA_KV_RANK, -1).astype(BF16)
    wv_pad = jnp.concatenate([wkv[:, :, MLA_NOPE:], pad_kv], axis=2).reshape(MLA_KV_RANK, -1).astype(BF16)
    e = np.zeros((LANES, MLA_HEADS * LANES), np.float32)
    for h in range(MLA_HEADS):
        for r in range(MLA_ROPE):
            e[r, h * LANES + MLA_NOPE + r] = 1.0
    pad_g = jnp.zeros((LANES - MLA_QK,), F32)
    gq_pad = jnp.concatenate([q_head_norm, pad_g]).reshape(1, LANES)
    gk_pad = jnp.concatenate([k_head_norm, pad_g]).reshape(1, LANES)
    return w_in_r, wuq_pad, wk_pad, wv_pad, jnp.asarray(e, BF16), gq_pad, gk_pad


def _odd_weights(w_in, w_gate_fwd, w_gate_bwd):
    w_in_r = jnp.concatenate(
        [w_in, jnp.zeros((D_MODEL, LANES - 2 * GLA_GATE_RANK), w_in.dtype)], axis=1).astype(BF16)
    n = GLA_HEADS * GLA_DK
    zf = jnp.zeros((LANES - GLA_GATE_RANK, n), F32)
    wgf_pad = jnp.concatenate([w_gate_fwd, zf], axis=0).astype(BF16)
    zb0 = jnp.zeros((GLA_GATE_RANK, n), F32)
    zb1 = jnp.zeros((LANES - 2 * GLA_GATE_RANK, n), F32)
    wgb_pad = jnp.concatenate([zb0, w_gate_bwd, zb1], axis=0).astype(BF16)
    return w_in_r, wgf_pad, wgb_pad


EVEN_SPLITS = ((0, 384), (384, 640), (640, 1152), (1152, 1664), (1664, 2176), (2176, 2688), (2688, 2816))
ODD_SPLITS = ((0, 512), (512, 1024), (1024, 2048), (2048, 3072), (3072, 3200))


def kernel(x, positions, mix_norm_even, w_in_even, mla_q_norm, mla_kv_norm, mla_w_uq, mla_w_ukv, mla_q_head_norm, mla_k_head_norm, ret_theta_fwd, ret_theta_bwd, ret_out_norm, w_out_even, mix_norm_odd, w_in_odd, gla_w_gate_fwd, gla_b_gate_fwd, gla_w_gate_bwd, gla_b_gate_bwd, gla_out_norm, w_out_odd, ffn_norm, ffn_w_up, ffn_conv_w, ffn_conv_b, ffn_w_down):
    x = x.reshape(TOKENS, D_MODEL)
    pos_col = positions.reshape(TOKENS, 1)
    ret_inv, ret_sgn, mla_inv, mla_sgn = _rope_lane_constants()
    ret_cos, ret_sin = _rope_tables(pos_col, ret_inv, ret_sgn)
    mla_cos, mla_sin = _rope_tables(pos_col, mla_inv, mla_sgn)

    for layer in range(DEPTH):
        i = layer // 2
        if layer % 2 == 0:
            w_in_r, wuq_pad, wk_pad, wv_pad, e_mat, gq_pad, gk_pad = _even_weights(
                w_in_even[i], mla_w_uq[i], mla_w_ukv[i], mla_q_head_norm[i], mla_k_head_norm[i])
            cq, ckv, rq, rk, rv, rg, kr = _norm_proj(x, mix_norm_even[i], w_in_r, EVEN_SPLITS, "in_proj_even")
            q, k, v = _mla_prep(cq, ckv, kr, mla_cos, mla_sin,
                                mla_q_norm[i].reshape(1, -1), mla_kv_norm[i].reshape(1, -1),
                                wuq_pad, wk_pad, wv_pad, e_mat, gq_pad, gk_pad)
            a = _attention(q, k, v).reshape(TOKENS, MLA_HEADS * MLA_V)
            lg_f = jnp.log1p(-jnp.exp2(-ret_theta_fwd[i].astype(F32)))
            lg_b = jnp.log1p(-jnp.exp2(-ret_theta_bwd[i].astype(F32)))
            lane_of = lambda lg: jnp.repeat(lg, RET_DK).reshape(RET_HEADS // 2, 1, LANES)
            r = _retention(rq, rk, rv, rg, ret_cos, ret_sin, lane_of(lg_f), lane_of(lg_b),
                           ret_out_norm[i].reshape(1, -1))
            w_out = w_out_even[i].astype(BF16)
            n_a = MLA_HEADS * MLA_V
            x = _out_proj(x, (a, r), (w_out[:n_a], w_out[n_a:]), "out_proj_even")
        else:
            w_in_r, wgf_pad, wgb_pad = _odd_weights(w_in_odd[i], gla_w_gate_fwd[i], gla_w_gate_bwd[i])
            gq, gk, gv, gr, ga = _norm_proj(x, mix_norm_odd[i], w_in_r, ODD_SPLITS, "in_proj_odd")
            g = _gla(gq, gk, gv, gr, ga, wgf_pad, gla_b_gate_fwd[i].reshape(1, -1),
                     wgb_pad, gla_b_gate_bwd[i].reshape(1, -1), gla_out_norm[i].reshape(1, -1))
            x = _out_proj(x, (g,), (w_out_odd[i].astype(BF16),), "out_proj_odd")
        x = _conv_ffn(x, ffn_norm[layer], ffn_w_up[layer].astype(BF16), ffn_conv_w[layer],
                      ffn_conv_b[layer], ffn_w_down[layer].astype(BF16))
    return x.reshape(BATCH, SEQ, D_MODEL)
```

```python
import functools

import numpy as np
import jax
import jax.numpy as jnp
from jax import lax
from jax.experimental import pallas as pl
from jax.experimental.pallas import tpu as pltpu

F32 = jnp.float32
BF16 = jnp.bfloat16

D_MODEL = 1024
BATCH = 4
SEQ = 8192
DEPTH = 4
TOKENS = BATCH * SEQ

MLA_HEADS = 8
MLA_Q_RANK = 384
MLA_KV_RANK = 256
MLA_NOPE = 64
MLA_ROPE = 32
MLA_V = 64
MLA_QK = MLA_NOPE + MLA_ROPE

RET_HEADS = 8
RET_DK = 64
RET_DV = 64
RET_CHUNK = 128

GLA_HEADS = 4
GLA_DK = 128
GLA_DV = 256
GLA_GATE_RANK = 16
GLA_TAU = 16.0
GLA_CHUNK = 64

D_FF = 2816
ROPE_THETA = 10000.0
EPS = 1e-6
LOG2E = 1.4426950408889634

LANES = 128
SUBLANES = 8
VMEM_LIMIT = 56 * 2**20

TM = 512
T_ATT_Q = 256
T_ATT_K = 512
T_SCAN = 512


def _params(*sem):
    return pltpu.CompilerParams(dimension_semantics=sem, vmem_limit_bytes=VMEM_LIMIT)


def _rms(x, g):
    return x * lax.rsqrt(jnp.mean(x * x, axis=-1, keepdims=True) + EPS) * g


def _dot(a, b):
    return jnp.dot(a, b, preferred_element_type=F32)


def _dot_nt(a, b):
    return lax.dot_general(a, b, (((1,), (1,)), ((), ())), preferred_element_type=F32)


def _dot_tn(a, b):
    return lax.dot_general(a, b, (((0,), (0,)), ((), ())), preferred_element_type=F32)


def _silu(x):
    return x * jax.nn.sigmoid(x)


def _rope_tables_kernel(pos_ref, inv_ref, sgn_ref, cos_ref, sin_ref):
    ang = pos_ref[...].astype(F32) * inv_ref[...]
    cos_ref[...] = jnp.cos(ang)
    sin_ref[...] = jnp.sin(ang) * sgn_ref[...]


def _rope_tables(pos_col, inv_lane, sgn_lane):
    return pl.pallas_call(
        _rope_tables_kernel,
        grid=(TOKENS // TM,),
        in_specs=[pl.BlockSpec((TM, 1), lambda i: (i, 0)),
                  pl.BlockSpec((1, LANES), lambda i: (0, 0)),
                  pl.BlockSpec((1, LANES), lambda i: (0, 0))],
        out_specs=[pl.BlockSpec((TM, LANES), lambda i: (i, 0))] * 2,
        out_shape=[jax.ShapeDtypeStruct((TOKENS, LANES), F32)] * 2,
        compiler_params=_params("parallel"),
        name="rope_tables",
    )(pos_col, inv_lane, sgn_lane)


def _rope_lane_constants():
    lane = np.arange(LANES)
    half = RET_DK // 2
    inv = ROPE_THETA ** (-np.arange(half, dtype=np.float32) / half)
    ret_inv = inv[lane % half].astype(np.float32)
    ret_sgn = np.where(lane % RET_DK < half, -1.0, 1.0).astype(np.float32)
    half = MLA_ROPE // 2
    inv = ROPE_THETA ** (-np.arange(half, dtype=np.float32) / half)
    in_rope = (lane >= MLA_NOPE) & (lane < MLA_QK)
    mla_inv = np.where(in_rope, inv[(lane - MLA_NOPE) % half], 0.0).astype(np.float32)
    mla_sgn = np.where(in_rope, np.where(lane < MLA_NOPE + half, -1.0, 1.0), 0.0).astype(np.float32)
    f = lambda a: jnp.asarray(a).reshape(1, LANES)
    return f(ret_inv), f(ret_sgn), f(mla_inv), f(mla_sgn)


def _norm_proj_kernel(x_ref, g_ref, w_ref, *out_refs, splits, chunk):
    h = _rms(x_ref[...], g_ref[...]).astype(BF16)
    for o_ref, (a, b) in zip(out_refs, splits):
        for c in range(a, b, chunk):
            e = min(c + chunk, b)
            o_ref[:, c - a:e - a] = _dot(h, w_ref[:, c:e]).astype(o_ref.dtype)


def _norm_proj(x, g, w, splits, name):
    n = w.shape[1]
    return pl.pallas_call(
        functools.partial(_norm_proj_kernel, splits=splits, chunk=512),
        grid=(TOKENS // TM,),
        in_specs=[pl.BlockSpec((TM, D_MODEL), lambda i: (i, 0)),
                  pl.BlockSpec((1, D_MODEL), lambda i: (0, 0)),
                  pl.BlockSpec((D_MODEL, n), lambda i: (0, 0))],
        out_specs=[pl.BlockSpec((TM, b - a), lambda i: (i, 0)) for a, b in splits],
        out_shape=[jax.ShapeDtypeStruct((TOKENS, b - a), BF16) for a, b in splits],
        compiler_params=_params("parallel"),
        name=name,
    )(x, g.reshape(1, D_MODEL), w)


def _out_proj_kernel(*refs, n_in):
    x_ref = refs[0]
    acts = refs[1:1 + n_in]
    ws = refs[1 + n_in:1 + 2 * n_in]
    o_ref = refs[1 + 2 * n_in]
    acc = x_ref[...]
    for a_ref, w_ref in zip(acts, ws):
        acc = acc + _dot(a_ref[...], w_ref[...])
    o_ref[...] = acc


def _out_proj(x, acts, ws, name):
    n_in = len(acts)
    return pl.pallas_call(
        functools.partial(_out_proj_kernel, n_in=n_in),
        grid=(TOKENS // TM,),
        in_specs=([pl.BlockSpec((TM, D_MODEL), lambda i: (i, 0))]
                  + [pl.BlockSpec((TM, a.shape[1]), lambda i: (i, 0)) for a in acts]
                  + [pl.BlockSpec(w.shape, lambda i: (0, 0)) for w in ws]),
        out_specs=pl.BlockSpec((TM, D_MODEL), lambda i: (i, 0)),
        out_shape=jax.ShapeDtypeStruct((TOKENS, D_MODEL), F32),
        compiler_params=_params("parallel"),
        name=name,
    )(x, *acts, *ws)


FF_CHUNK = 256


def _ffn_down_kernel(gate_ref, gprev_ref, gnext_ref, val_ref, cw_ref, cb_ref, wd_ref, x_ref, o_ref,
                     *, tiles_per_seq):
    j = pl.program_id(0) % tiles_per_seq
    first = j == 0
    last = j == tiles_per_seq - 1
    row = lax.broadcasted_iota(jnp.int32, (TM, 1), 0)
    acc = x_ref[...]
    for c in range(0, D_FF, FF_CHUNK):
        sl = slice(c, c + FF_CHUNK)
        g = gate_ref[:, sl].astype(F32)
        prev_row = jnp.where(first, 0.0, gprev_ref[SUBLANES - 1:SUBLANES, sl].astype(F32))
        next_row = jnp.where(last, 0.0, gnext_ref[0:1, sl].astype(F32))
        g_m1 = jnp.where(row == 0, prev_row, pltpu.roll(g, 1, 0))
        g_p1 = jnp.where(row == TM - 1, next_row, pltpu.roll(g, TM - 1, 0))
        conv = cw_ref[0:1, sl] * g_m1 + cw_ref[1:2, sl] * g + cw_ref[2:3, sl] * g_p1 + cb_ref[:, sl]
        act = (_silu(conv) * val_ref[:, sl].astype(F32)).astype(BF16)
        acc = acc + _dot(act, wd_ref[sl, :])
    o_ref[...] = acc


def _ffn_down(gate, val, conv_w, conv_b, w_down, x):
    tiles_per_seq = SEQ // TM
    rows8 = TM // SUBLANES
    n8 = TOKENS // SUBLANES
    return pl.pallas_call(
        functools.partial(_ffn_down_kernel, tiles_per_seq=tiles_per_seq),
        grid=(TOKENS // TM,),
        in_specs=[pl.BlockSpec((TM, D_FF), lambda i: (i, 0)),
                  pl.BlockSpec((SUBLANES, D_FF), lambda i: (jnp.maximum(i * rows8 - 1, 0), 0)),
                  pl.BlockSpec((SUBLANES, D_FF), lambda i: (jnp.minimum((i + 1) * rows8, n8 - 1), 0)),
                  pl.BlockSpec((TM, D_FF), lambda i: (i, 0)),
                  pl.BlockSpec((3, D_FF), lambda i: (0, 0)),
                  pl.BlockSpec((1, D_FF), lambda i: (0, 0)),
                  pl.BlockSpec((D_FF, D_MODEL), lambda i: (0, 0)),
                  pl.BlockSpec((TM, D_MODEL), lambda i: (i, 0))],
        out_specs=pl.BlockSpec((TM, D_MODEL), lambda i: (i, 0)),
        out_shape=jax.ShapeDtypeStruct((TOKENS, D_MODEL), F32),
        compiler_params=_params("parallel"),
        name="ffn_down",
    )(gate, gate, gate, val, conv_w, conv_b.reshape(1, D_FF), w_down, x)


def _conv_ffn(x, norm_g, w_up, conv_w, conv_b, w_down):
    gate, val = _norm_proj(x, norm_g, w_up, ((0, D_FF), (D_FF, 2 * D_FF)), "ffn_up")
    return _ffn_down(gate, val, conv_w, conv_b, w_down, x)


def _mla_prep_kernel(cq_ref, ckv_ref, kr_ref, cos_ref, sin_ref, qn_ref, kvn_ref, wuq_ref, wk_ref, wvt_ref,
                     e_ref, gq_ref, gk_ref, q_out, k_out, vt_out):
    cos = cos_ref[...]
    sin = sin_ref[...]
    lane = lax.broadcasted_iota(jnp.int32, (1, LANES), 1)
    take_upper = lane < MLA_NOPE + MLA_ROPE // 2
    ones_row = (lax.broadcasted_iota(jnp.int32, (LANES, 1), 0) == MLA_V).astype(F32)
    cqn = _rms(cq_ref[...].astype(F32), qn_ref[...]).astype(BF16)
    ckvn = _rms(ckv_ref[...].astype(F32), kvn_ref[...]).astype(BF16)
    qf = _dot(cqn, wuq_ref[...])
    kf = _dot(ckvn, wk_ref[...]) + _dot(kr_ref[...], e_ref[...])
    scale = MLA_QK ** -0.5 * LOG2E

    def head_norm_rope(xh, g):
        xh = xh * lax.rsqrt(jnp.sum(xh * xh, axis=-1, keepdims=True) * (1.0 / MLA_QK) + EPS) * g
        half = MLA_ROPE // 2
        rolled = jnp.where(take_upper, pltpu.roll(xh, LANES - half, 1), pltpu.roll(xh, half, 1))
        return xh * cos + rolled * sin

    for h in range(MLA_HEADS):
        sl = slice(h * LANES, (h + 1) * LANES)
        q_out[0, h] = (head_norm_rope(qf[:, sl], gq_ref[...]) * scale).astype(BF16)
        k_out[0, h] = head_norm_rope(kf[:, sl], gk_ref[...]).astype(BF16)
        vt = _dot_nt(wvt_ref[sl, :], ckvn)
        vt_out[0, h, 0] = (vt + ones_row).astype(BF16)


def _mla_prep(cq, ckv, kr, cos, sin, q_norm, kv_norm, wuq_pad, wk_pad, wvt_pad, e_mat, gq_pad, gk_pad):
    assert TM == T_ATT_K
    tps = SEQ // TM
    tok = lambda n: pl.BlockSpec((TM, n), lambda i: (i, 0))
    full = lambda a: pl.BlockSpec(a.shape, lambda i: (0,) * a.ndim)
    head_major = pl.BlockSpec((1, MLA_HEADS, TM, LANES), lambda i: (i // tps, 0, i % tps, 0))
    head_major_t = pl.BlockSpec((1, MLA_HEADS, 1, LANES, TM), lambda i: (i // tps, 0, i % tps, 0, 0))
    args = (cq, ckv, kr, cos, sin, q_norm, kv_norm, wuq_pad, wk_pad, wvt_pad, e_mat, gq_pad, gk_pad)
    return pl.pallas_call(
        _mla_prep_kernel,
        grid=(TOKENS // TM,),
        in_specs=[tok(MLA_Q_RANK), tok(MLA_KV_RANK), tok(LANES), tok(LANES), tok(LANES)]
                 + [full(a) for a in args[5:]],
        out_specs=[head_major, head_major, head_major_t],
        out_shape=[jax.ShapeDtypeStruct((BATCH, MLA_HEADS, SEQ, LANES), BF16)] * 2
                  + [jax.ShapeDtypeStruct((BATCH, MLA_HEADS, SEQ // T_ATT_K, LANES, T_ATT_K), BF16)],
        compiler_params=_params("parallel"),
        name="mla_prep",
    )(*args)


def _attn_kernel(q_ref, k_ref, vt_ref, o_ref, m_ref, acc_ref, s_ref, bmax_ref):
    nkb = SEQ // T_ATT_K
    m_ref[...] = jnp.full(m_ref.shape, -jnp.inf, F32)
    acc_ref[...] = jnp.zeros(acc_ref.shape, F32)

    def scores(kb, slot):
        off = pl.multiple_of(kb * T_ATT_K, T_ATT_K)
        for hh in range(2):
            st = _dot_nt(k_ref[0, hh, pl.ds(off, T_ATT_K), :], q_ref[0, hh])
            s_ref[slot, hh] = st
            bmax_ref[slot, hh] = jnp.max(st, axis=0, keepdims=True)

    def consume(kb, slot):
        for hh in range(2):
            m_old = m_ref[hh]
            m_new = jnp.maximum(m_old, bmax_ref[slot, hh])
            p = jnp.exp2(s_ref[slot, hh] - m_new).astype(BF16)
            acc_ref[hh] = acc_ref[hh] * jnp.exp2(m_old - m_new) + _dot(vt_ref[0, hh, kb], p)
            m_ref[hh] = m_new

    scores(0, 0)

    def body(i, carry):
        kb = 2 * i
        scores(kb + 1, 1)
        consume(kb, 0)
        scores(kb + 2, 0)
        consume(kb + 1, 1)
        return carry

    lax.fori_loop(0, nkb // 2 - 1, body, 0)
    scores(nkb - 1, 1)
    consume(nkb - 2, 0)
    consume(nkb - 1, 1)
    lane = lax.broadcasted_iota(jnp.int32, (1, LANES), 1)
    outs = []
    for hh in range(2):
        acc = acc_ref[hh].T
        outs.append(acc / acc[:, MLA_V:MLA_V + 1])
    o = jnp.where(lane < MLA_V, outs[0], pltpu.roll(outs[1], MLA_V, 1))
    o_ref[0] = o.astype(BF16)


def _attention(q, k, vt):
    nkb = SEQ // T_ATT_K
    return pl.pallas_call(
        _attn_kernel,
        grid=(BATCH, MLA_HEADS // 2, SEQ // T_ATT_Q),
        in_specs=[pl.BlockSpec((1, 2, T_ATT_Q, LANES), lambda b, h, i: (b, h, i, 0)),
                  pl.BlockSpec((1, 2, SEQ, LANES), lambda b, h, i: (b, h, 0, 0)),
                  pl.BlockSpec((1, 2, nkb, LANES, T_ATT_K), lambda b, h, i: (b, h, 0, 0, 0))],
        out_specs=pl.BlockSpec((1, T_ATT_Q, LANES), lambda b, h, i: (b, i, h)),
        out_shape=jax.ShapeDtypeStruct((BATCH, SEQ, MLA_HEADS * MLA_V), BF16),
        scratch_shapes=[pltpu.VMEM((2, 1, T_ATT_Q), F32), pltpu.VMEM((2, LANES, T_ATT_Q), F32),
                        pltpu.VMEM((2, 2, T_ATT_K, T_ATT_Q), F32), pltpu.VMEM((2, 2, 1, T_ATT_Q), F32)],
        compiler_params=_params("parallel", "parallel", "parallel"),
        name="mla_attention",
    )(q, k, vt)


def _ret_rope(x, cos, sin, lower_half):
    half = RET_DK // 2
    rolled = jnp.where(lower_half, pltpu.roll(x, LANES - half, 1), pltpu.roll(x, half, 1))
    return x * cos + rolled * sin


def _ret_common():
    lane = lax.broadcasted_iota(jnp.int32, (1, LANES), 1)
    rowi = lax.broadcasted_iota(jnp.int32, (RET_CHUNK, 1), 0)
    head0 = lane < RET_DK
    same_head = (rowi < RET_DK) == head0
    lower_half = (lane % RET_DK) < RET_DK // 2
    return lane, rowi, head0, same_head, lower_half


def _ret_fwd_kernel(q_ref, k_ref, v_ref, cos_ref, sin_ref, lgf_ref, lgb_ref, o_ref, state_ref):
    @pl.when(pl.program_id(2) == 0)
    def _():
        state_ref[...] = jnp.zeros_like(state_ref)

    lane, rowi, head0, same_head, lower_half = _ret_common()
    C = RET_CHUNK
    row = rowi.astype(F32)
    col = lane.astype(F32)
    lgf = lgf_ref[0]
    lgb = lgb_ref[0]
    rel = row - col

    def decay(lf, lb):
        return jnp.exp(jnp.where(rel >= 0, lf * rel, -lb * rel))

    d0 = decay(lgf[:, 0:1], lgb[:, 0:1])
    d1 = decay(lgf[:, RET_DK:RET_DK + 1], lgb[:, RET_DK:RET_DK + 1])
    xi = jnp.exp(lgf * (row + 1.0))
    zeta = jnp.exp(lgf * (C - 1.0 - row))
    chunk_decay = jnp.exp(lgf * C)
    for c in range(T_SCAN // C):
        sl = pl.ds(c * C, C)
        cos = cos_ref[sl, :]
        sin = sin_ref[sl, :]
        q = _ret_rope(q_ref[sl, :].astype(F32), cos, sin, lower_half)
        k = _ret_rope(k_ref[sl, :].astype(F32), cos, sin, lower_half) * (RET_DK ** -0.5)
        v = v_ref[sl, :]
        kb = k.astype(BF16)
        s0 = _dot_nt(jnp.where(head0, q, 0.0).astype(BF16), kb) * d0
        s1 = _dot_nt(jnp.where(head0, 0.0, q).astype(BF16), kb) * d1
        intra = jnp.where(head0, _dot(s0.astype(BF16), v), _dot(s1.astype(BF16), v))
        state = state_ref[...]
        cross = _dot(q.astype(BF16), state.astype(BF16)) * xi
        o_ref[sl, :] = intra + cross
        upd = _dot_tn((k * zeta).astype(BF16), v)
        state_ref[...] = state * chunk_decay + jnp.where(same_head, upd, 0.0)


def _ret_bwd_kernel(q_ref, k_ref, v_ref, g_ref, o1_ref, cos_ref, sin_ref, lgb_ref, gain_ref, o_ref, state_ref):
    @pl.when(pl.program_id(2) == 0)
    def _():
        state_ref[...] = jnp.zeros_like(state_ref)

    lane, rowi, head0, same_head, lower_half = _ret_common()
    C = RET_CHUNK
    row = rowi.astype(F32)
    lgb = lgb_ref[0]
    xi = jnp.exp(lgb * (C - row))
    zeta = jnp.exp(lgb * row)
    chunk_decay = jnp.exp(lgb * C)
    for c in reversed(range(T_SCAN // C)):
        sl = pl.ds(c * C, C)
        cos = cos_ref[sl, :]
        sin = sin_ref[sl, :]
        q = _ret_rope(q_ref[sl, :].astype(F32), cos, sin, lower_half)
        k = _ret_rope(k_ref[sl, :].astype(F32), cos, sin, lower_half) * (RET_DK ** -0.5)
        v = v_ref[sl, :]
        state = state_ref[...]
        o = o1_ref[sl, :] + _dot(q.astype(BF16), state.astype(BF16)) * xi
        o2 = o * o
        ss0 = jnp.sum(jnp.where(head0, o2, 0.0), axis=-1, keepdims=True)
        ss1 = jnp.sum(jnp.where(head0, 0.0, o2), axis=-1, keepdims=True)
        ms = jnp.where(head0, ss0, ss1) * (1.0 / RET_DV)
        y = o * lax.rsqrt(ms + EPS) * gain_ref[...]
        o_ref[sl, :] = (_silu(g_ref[sl, :].astype(F32)) * y).astype(BF16)
        upd = _dot_tn((k * zeta).astype(BF16), v)
        state_ref[...] = state * chunk_decay + jnp.where(same_head, upd, 0.0)


def _retention(rq, rk, rv, rg, cos, sin, lgf_lane, lgb_lane, gain):
    nt = SEQ // T_SCAN
    pairs = RET_HEADS // 2
    fwd_tok = lambda b, p, t: (b * nt + t, p)
    fwd_tab = lambda b, p, t: (b * nt + t, 0)
    bwd_tok = lambda b, p, t: (b * nt + nt - 1 - t, p)
    bwd_tab = lambda b, p, t: (b * nt + nt - 1 - t, 0)
    per_pair = lambda b, p, t: (p, 0, 0)
    blk = (T_SCAN, LANES)
    o1 = pl.pallas_call(
        _ret_fwd_kernel,
        grid=(BATCH, pairs, nt),
        in_specs=[pl.BlockSpec(blk, fwd_tok)] * 3 + [pl.BlockSpec(blk, fwd_tab)] * 2
                 + [pl.BlockSpec((1, 1, LANES), per_pair)] * 2,
        out_specs=pl.BlockSpec(blk, fwd_tok),
        out_shape=jax.ShapeDtypeStruct((TOKENS, RET_HEADS * RET_DV), F32),
        scratch_shapes=[pltpu.VMEM((LANES, LANES), F32)],
        compiler_params=_params("parallel", "parallel", "arbitrary"),
        name="retention_fwd",
    )(rq, rk, rv, cos, sin, lgf_lane, lgb_lane)
    return pl.pallas_call(
        _ret_bwd_kernel,
        grid=(BATCH, pairs, nt),
        in_specs=[pl.BlockSpec(blk, bwd_tok)] * 5 + [pl.BlockSpec(blk, bwd_tab)] * 2
                 + [pl.BlockSpec((1, 1, LANES), per_pair), pl.BlockSpec((1, LANES), lambda b, p, t: (0, p))],
        out_specs=pl.BlockSpec(blk, bwd_tok),
        out_shape=jax.ShapeDtypeStruct((TOKENS, RET_HEADS * RET_DV), BF16),
        scratch_shapes=[pltpu.VMEM((LANES, LANES), F32)],
        compiler_params=_params("parallel", "parallel", "arbitrary"),
        name="retention_bwd",
    )(rq, rk, rv, rg, o1, cos, sin, lgb_lane, gain)


def _log_sigmoid(z):
    return jnp.minimum(z, 0.0) - jnp.log(1.0 + jnp.exp(-jnp.abs(z)))


def _chunk_cumsum(tri, la):
    hi = la.astype(BF16)
    lo = (la - hi.astype(F32)).astype(BF16)
    return _dot(tri, hi) + _dot(tri, lo)


def _gla_fwd_kernel(q_ref, k_ref, v_ref, ga_ref, wg_ref, bg_ref, o_ref, state_ref):
    @pl.when(pl.program_id(2) == 0)
    def _():
        state_ref[...] = jnp.zeros_like(state_ref)

    C = GLA_CHUNK
    ri = lax.broadcasted_iota(jnp.int32, (C, C), 0)
    ci = lax.broadcasted_iota(jnp.int32, (C, C), 1)
    lower = ri >= ci
    tri = lower.astype(BF16)
    la_all = _log_sigmoid(_dot(ga_ref[...], wg_ref[...]) + bg_ref[...]) * (1.0 / GLA_TAU)
    for c in range(T_SCAN // C):
        sl = pl.ds(c * C, C)
        b = _chunk_cumsum(tri, la_all[c * C:(c + 1) * C])
        b_mid = b[C // 2:C // 2 + 1]
        b_last = b[C - 1:C]
        q = q_ref[sl, :].astype(F32) * (GLA_DK ** -0.5)
        k = k_ref[sl, :].astype(F32)
        v = v_ref[sl, :]
        a = _dot_nt((q * jnp.exp(b - b_mid)).astype(BF16), (k * jnp.exp(b_mid - b)).astype(BF16))
        a = jnp.where(lower, a, 0.0)
        state = state_ref[...]
        inter = _dot_nt((q * jnp.exp(b)).astype(BF16), state.astype(BF16))
        o_ref[sl, :] = _dot(a.astype(BF16), v) + inter
        k_dec = (k * jnp.exp(b_last - b)).astype(BF16)
        state_ref[...] = state * jnp.exp(b_last) + _dot_tn(v, k_dec)


def _gla_bwd_kernel(q_ref, k_ref, v_ref, r_ref, o1_ref, ga_ref, wg_ref, bg_ref, gain_ref, o_ref, state_ref):
    @pl.when(pl.program_id(2) == 0)
    def _():
        state_ref[...] = jnp.zeros_like(state_ref)

    C = GLA_CHUNK
    ri = lax.broadcasted_iota(jnp.int32, (C, C), 0)
    ci = lax.broadcasted_iota(jnp.int32, (C, C), 1)
    upper_incl = (ci >= ri).astype(BF16)
    upper_strict = ci > ri
    la_all = _log_sigmoid(_dot(ga_ref[...], wg_ref[...]) + bg_ref[...]) * (1.0 / GLA_TAU)
    for c in reversed(range(T_SCAN // C)):
        sl = pl.ds(c * C, C)
        b = _chunk_cumsum(upper_incl, la_all[c * C:(c + 1) * C])
        b_mid = b[C // 2 - 1:C // 2]
        b_last = b[0:1]
        q = q_ref[sl, :].astype(F32) * (GLA_DK ** -0.5)
        k = k_ref[sl, :].astype(F32)
        v = v_ref[sl, :]
        a = _dot_nt((q * jnp.exp(b - b_mid)).astype(BF16), (k * jnp.exp(b_mid - b)).astype(BF16))
        a = jnp.where(upper_strict, a, 0.0)
        state = state_ref[...]
        inter = _dot_nt((q * jnp.exp(b)).astype(BF16), state.astype(BF16))
        o = o1_ref[sl, :] + _dot(a.astype(BF16), v) + inter
        y = _rms(o, gain_ref[...])
        o_ref[sl, :] = (_silu(r_ref[sl, :].astype(F32)) * y).astype(BF16)
        k_dec = (k * jnp.exp(b_last - b)).astype(BF16)
        state_ref[...] = state * jnp.exp(b_last) + _dot_tn(v, k_dec)


def _gla(gq, gk, gv, gr, ga, wgf_pad, bgf, wgb_pad, bgb, gain):
    nt = SEQ // T_SCAN
    fwd_tok = lambda b, h, t: (b * nt + t, h)
    fwd_tab = lambda b, h, t: (b * nt + t, 0)
    bwd_tok = lambda b, h, t: (b * nt + nt - 1 - t, h)
    bwd_tab = lambda b, h, t: (b * nt + nt - 1 - t, 0)
    per_head = lambda b, h, t: (0, h)
    kblk = (T_SCAN, GLA_DK)
    vblk = (T_SCAN, GLA_DV)
    o1 = pl.pallas_call(
        _gla_fwd_kernel,
        grid=(BATCH, GLA_HEADS, nt),
        in_specs=[pl.BlockSpec(kblk, fwd_tok), pl.BlockSpec(kblk, fwd_tok), pl.BlockSpec(vblk, fwd_tok),
                  pl.BlockSpec((T_SCAN, LANES), fwd_tab),
                  pl.BlockSpec((LANES, GLA_DK), per_head), pl.BlockSpec((1, GLA_DK), per_head)],
        out_specs=pl.BlockSpec(vblk, fwd_tok),
        out_shape=jax.ShapeDtypeStruct((TOKENS, GLA_HEADS * GLA_DV), F32),
        scratch_shapes=[pltpu.VMEM((GLA_DV, GLA_DK), F32)],
        compiler_params=_params("parallel", "parallel", "arbitrary"),
        name="gla_fwd",
    )(gq, gk, gv, ga, wgf_pad, bgf)
    return pl.pallas_call(
        _gla_bwd_kernel,
        grid=(BATCH, GLA_HEADS, nt),
        in_specs=[pl.BlockSpec(kblk, bwd_tok), pl.BlockSpec(kblk, bwd_tok), pl.BlockSpec(vblk, bwd_tok),
                  pl.BlockSpec(vblk, bwd_tok), pl.BlockSpec(vblk, bwd_tok),
                  pl.BlockSpec((T_SCAN, LANES), bwd_tab),
                  pl.BlockSpec((LANES, GLA_DK), per_head), pl.BlockSpec((1, GLA_DK), per_head),
                  pl.BlockSpec((1, GLA_DV), per_head)],
        out_specs=pl.BlockSpec(vblk, bwd_tok),
        out_shape=jax.ShapeDtypeStruct((TOKENS, GLA_HEADS * GLA_DV), BF16),
        scratch_shapes=[pltpu.VMEM((GLA_DV, GLA_DK), F32)],
        compiler_params=_params("parallel", "parallel", "arbitrary"),
        name="gla_bwd",
    )(gq, gk, gv, gr, o1, ga, wgb_pad, bgb, gain)


def _even_weights(w_in, w_uq, w_ukv, q_head_norm, k_head_norm):
    kr0 = MLA_Q_RANK + MLA_KV_RANK
    w_in_r = jnp.concatenate(
        [w_in[:, :kr0], w_in[:, kr0 + MLA_ROPE:], w_in[:, kr0:kr0 + MLA_ROPE],
         jnp.zeros((D_MODEL, LANES - MLA_ROPE), w_in.dtype)], axis=1).astype(BF16)
    pad_q = jnp.zeros((MLA_Q_RANK, MLA_HEADS, LANES - MLA_QK), F32)
    wuq_pad = jnp.concatenate([w_uq.reshape(MLA_Q_RANK, MLA_HEADS, MLA_QK), pad_q], axis=2)
    wuq_pad = wuq_pad.reshape(MLA_Q_RANK, MLA_HEADS * LANES).astype(BF16)
    wkv = w_ukv.reshape(MLA_KV_RANK, MLA_HEADS, MLA_NOPE + MLA_V)
    pad_kv = jnp.zeros((MLA_KV_RANK, MLA_HEADS, LANES - MLA_NOPE), F32)
    wk_pad = jnp.concatenate([wkv[:, :, :MLA_NOPE], pad_kv], axis=2).reshape(MLA_KV_RANK, -1).astype(BF16)
    wv_pad = jnp.concatenate([wkv[:, :, MLA_NOPE:], pad_kv], axis=2).reshape(MLA_KV_RANK, -1)
    wvt_pad = wv_pad.T.astype(BF16)
    e = np.zeros((LANES, MLA_HEADS * LANES), np.float32)
    for h in range(MLA_HEADS):
        for r in range(MLA_ROPE):
            e[r, h * LANES + MLA_NOPE + r] = 1.0
    pad_g = jnp.zeros((LANES - MLA_QK,), F32)
    gq_pad = jnp.concatenate([q_head_norm, pad_g]).reshape(1, LANES)
    gk_pad = jnp.concatenate([k_head_norm, pad_g]).reshape(1, LANES)
    return w_in_r, wuq_pad, wk_pad, wvt_pad, jnp.asarray(e, BF16), gq_pad, gk_pad


def _odd_weights(w_in, w_gate_fwd, w_gate_bwd):
    w_in_r = jnp.concatenate(
        [w_in, jnp.zeros((D_MODEL, LANES - 2 * GLA_GATE_RANK), w_in.dtype)], axis=1).astype(BF16)
    n = GLA_HEADS * GLA_DK
    zf = jnp.zeros((LANES - GLA_GATE_RANK, n), F32)
    wgf_pad = jnp.concatenate([w_gate_fwd, zf], axis=0).astype(BF16)
    zb0 = jnp.zeros((GLA_GATE_RANK, n), F32)
    zb1 = jnp.zeros((LANES - 2 * GLA_GATE_RANK, n), F32)
    wgb_pad = jnp.concatenate([zb0, w_gate_bwd, zb1], axis=0).astype(BF16)
    return w_in_r, wgf_pad, wgb_pad


EVEN_SPLITS = ((0, 384), (384, 640), (640, 1152), (1152, 1664), (1664, 2176), (2176, 2688), (2688, 2816))
ODD_SPLITS = ((0, 512), (512, 1024), (1024, 2048), (2048, 3072), (3072, 3200))


def kernel(x, positions, mix_norm_even, w_in_even, mla_q_norm, mla_kv_norm, mla_w_uq, mla_w_ukv, mla_q_head_norm, mla_k_head_norm, ret_theta_fwd, ret_theta_bwd, ret_out_norm, w_out_even, mix_norm_odd, w_in_odd, gla_w_gate_fwd, gla_b_gate_fwd, gla_w_gate_bwd, gla_b_gate_bwd, gla_out_norm, w_out_odd, ffn_norm, ffn_w_up, ffn_conv_w, ffn_conv_b, ffn_w_down):
    x = x.reshape(TOKENS, D_MODEL)
    pos_col = positions.reshape(TOKENS, 1)
    ret_inv, ret_sgn, mla_inv, mla_sgn = _rope_lane_constants()
    ret_cos, ret_sin = _rope_tables(pos_col, ret_inv, ret_sgn)
    mla_cos, mla_sin = _rope_tables(pos_col, mla_inv, mla_sgn)

    for layer in range(DEPTH):
        i = layer // 2
        if layer % 2 == 0:
            w_in_r, wuq_pad, wk_pad, wvt_pad, e_mat, gq_pad, gk_pad = _even_weights(
                w_in_even[i], mla_w_uq[i], mla_w_ukv[i], mla_q_head_norm[i], mla_k_head_norm[i])
            cq, ckv, rq, rk, rv, rg, kr = _norm_proj(x, mix_norm_even[i], w_in_r, EVEN_SPLITS, "in_proj_even")
            q, k, v = _mla_prep(cq, ckv, kr, mla_cos, mla_sin,
                                mla_q_norm[i].reshape(1, -1), mla_kv_norm[i].reshape(1, -1),
                                wuq_pad, wk_pad, wvt_pad, e_mat, gq_pad, gk_pad)
            a = _attention(q, k, v).reshape(TOKENS, MLA_HEADS * MLA_V)
            lg_f = jnp.log1p(-jnp.exp2(-ret_theta_fwd[i].astype(F32)))
            lg_b = jnp.log1p(-jnp.exp2(-ret_theta_bwd[i].astype(F32)))
            lane_of = lambda lg: jnp.repeat(lg, RET_DK).reshape(RET_HEADS // 2, 1, LANES)
            r = _retention(rq, rk, rv, rg, ret_cos, ret_sin, lane_of(lg_f), lane_of(lg_b),
                           ret_out_norm[i].reshape(1, -1))
            w_out = w_out_even[i].astype(BF16)
            n_a = MLA_HEADS * MLA_V
            x = _out_proj(x, (a, r), (w_out[:n_a], w_out[n_a:]), "out_proj_even")
        else:
            w_in_r, wgf_pad, wgb_pad = _odd_weights(w_in_odd[i], gla_w_gate_fwd[i], gla_w_gate_bwd[i])
            gq, gk, gv, gr, ga = _norm_proj(x, mix_norm_odd[i], w_in_r, ODD_SPLITS, "in_proj_odd")
            g = _gla(gq, gk, gv, gr, ga, wgf_pad, gla_b_gate_fwd[i].reshape(1, -1),
                     wgb_pad, gla_b_gate_bwd[i].reshape(1, -1), gla_out_norm[i].reshape(1, -1))
            x = _out_proj(x, (g,), (w_out_odd[i].astype(BF16),), "out_proj_odd")
        x = _conv_ffn(x, ffn_norm[layer], ffn_w_up[layer].astype(BF16), ffn_conv_w[layer],
                      ffn_conv_b[layer], ffn_w_down[layer].astype(BF16))
    return x.reshape(BATCH, SEQ, D_MODEL)
```

```python
import functools

import numpy as np
import jax
import jax.numpy as jnp
from jax import lax
from jax.experimental import pallas as pl
from jax.experimental.pallas import tpu as pltpu

F32 = jnp.float32
BF16 = jnp.bfloat16

D_MODEL = 1024
BATCH = 4
SEQ = 8192
DEPTH = 4
TOKENS = BATCH * SEQ

MLA_HEADS = 8
MLA_Q_RANK = 384
MLA_KV_RANK = 256
MLA_NOPE = 64
MLA_ROPE = 32
MLA_V = 64
MLA_QK = MLA_NOPE + MLA_ROPE

RET_HEADS = 8
RET_DK = 64
RET_DV = 64
RET_CHUNK = 128

GLA_HEADS = 4
GLA_DK = 128
GLA_DV = 256
GLA_GATE_RANK = 16
GLA_TAU = 16.0
GLA_CHUNK = 64

D_FF = 2816
ROPE_THETA = 10000.0
EPS = 1e-6
LOG2E = 1.4426950408889634

LANES = 128
SUBLANES = 8
VMEM_LIMIT = 56 * 2**20

TM = 512
T_ATT_Q = 256
T_ATT_K = 512
VT_ROWS = 80
T_SCAN = 512


def _params(*sem):
    return pltpu.CompilerParams(dimension_semantics=sem, vmem_limit_bytes=VMEM_LIMIT)


def _rms(x, g):
    return x * lax.rsqrt(jnp.mean(x * x, axis=-1, keepdims=True) + EPS) * g


def _dot(a, b):
    return jnp.dot(a, b, preferred_element_type=F32)


def _dot_nt(a, b):
    return lax.dot_general(a, b, (((1,), (1,)), ((), ())), preferred_element_type=F32)


def _dot_tn(a, b):
    return lax.dot_general(a, b, (((0,), (0,)), ((), ())), preferred_element_type=F32)


def _silu(x):
    return x * jax.nn.sigmoid(x)


def _rope_tables_kernel(pos_ref, inv_ref, sgn_ref, cos_ref, sin_ref):
    ang = pos_ref[...].astype(F32) * inv_ref[...]
    cos_ref[...] = jnp.cos(ang)
    sin_ref[...] = jnp.sin(ang) * sgn_ref[...]


def _rope_tables(pos_col, inv_lane, sgn_lane):
    return pl.pallas_call(
        _rope_tables_kernel,
        grid=(TOKENS // TM,),
        in_specs=[pl.BlockSpec((TM, 1), lambda i: (i, 0)),
                  pl.BlockSpec((1, LANES), lambda i: (0, 0)),
                  pl.BlockSpec((1, LANES), lambda i: (0, 0))],
        out_specs=[pl.BlockSpec((TM, LANES), lambda i: (i, 0))] * 2,
        out_shape=[jax.ShapeDtypeStruct((TOKENS, LANES), F32)] * 2,
        compiler_params=_params("parallel"),
        name="rope_tables",
    )(pos_col, inv_lane, sgn_lane)


def _rope_lane_constants():
    lane = np.arange(LANES)
    half = RET_DK // 2
    inv = ROPE_THETA ** (-np.arange(half, dtype=np.float32) / half)
    ret_inv = inv[lane % half].astype(np.float32)
    ret_sgn = np.where(lane % RET_DK < half, -1.0, 1.0).astype(np.float32)
    half = MLA_ROPE // 2
    inv = ROPE_THETA ** (-np.arange(half, dtype=np.float32) / half)
    in_rope = (lane >= MLA_NOPE) & (lane < MLA_QK)
    mla_inv = np.where(in_rope, inv[(lane - MLA_NOPE) % half], 0.0).astype(np.float32)
    mla_sgn = np.where(in_rope, np.where(lane < MLA_NOPE + half, -1.0, 1.0), 0.0).astype(np.float32)
    f = lambda a: jnp.asarray(a).reshape(1, LANES)
    return f(ret_inv), f(ret_sgn), f(mla_inv), f(mla_sgn)


def _norm_proj_kernel(x_ref, g_ref, w_ref, *out_refs, splits, chunk):
    h = _rms(x_ref[...], g_ref[...]).astype(BF16)
    for o_ref, (a, b) in zip(out_refs, splits):
        for c in range(a, b, chunk):
            e = min(c + chunk, b)
            o_ref[:, c - a:e - a] = _dot(h, w_ref[:, c:e]).astype(o_ref.dtype)


def _norm_proj(x, g, w, splits, name):
    n = w.shape[1]
    return pl.pallas_call(
        functools.partial(_norm_proj_kernel, splits=splits, chunk=512),
        grid=(TOKENS // TM,),
        in_specs=[pl.BlockSpec((TM, D_MODEL), lambda i: (i, 0)),
                  pl.BlockSpec((1, D_MODEL), lambda i: (0, 0)),
                  pl.BlockSpec((D_MODEL, n), lambda i: (0, 0))],
        out_specs=[pl.BlockSpec((TM, b - a), lambda i: (i, 0)) for a, b in splits],
        out_shape=[jax.ShapeDtypeStruct((TOKENS, b - a), BF16) for a, b in splits],
        compiler_params=_params("parallel"),
        name=name,
    )(x, g.reshape(1, D_MODEL), w)


def _out_proj_kernel(*refs, n_in):
    x_ref = refs[0]
    acts = refs[1:1 + n_in]
    ws = refs[1 + n_in:1 + 2 * n_in]
    o_ref = refs[1 + 2 * n_in]
    acc = x_ref[...]
    for a_ref, w_ref in zip(acts, ws):
        acc = acc + _dot(a_ref[...], w_ref[...])
    o_ref[...] = acc


def _out_proj(x, acts, ws, name):
    n_in = len(acts)
    return pl.pallas_call(
        functools.partial(_out_proj_kernel, n_in=n_in),
        grid=(TOKENS // TM,),
        in_specs=([pl.BlockSpec((TM, D_MODEL), lambda i: (i, 0))]
                  + [pl.BlockSpec((TM, a.shape[1]), lambda i: (i, 0)) for a in acts]
                  + [pl.BlockSpec(w.shape, lambda i: (0, 0)) for w in ws]),
        out_specs=pl.BlockSpec((TM, D_MODEL), lambda i: (i, 0)),
        out_shape=jax.ShapeDtypeStruct((TOKENS, D_MODEL), F32),
        compiler_params=_params("parallel"),
        name=name,
    )(x, *acts, *ws)


FF_CHUNK = 256
HALO = 16


def _ffn_up_kernel(x_ref, xprev_ref, xnext_ref, g_ref, w_ref, cw_ref, cb_ref, act_ref, h_ref, *, tiles_per_seq):
    j = pl.program_id(0) % tiles_per_seq
    gn = g_ref[...]
    keep_prev = (j != 0).astype(F32)
    keep_next = (j != tiles_per_seq - 1).astype(F32)
    h_ref[0:HALO, :] = (_rms(xprev_ref[...], gn) * keep_prev).astype(BF16)
    h_ref[HALO:HALO + TM, :] = _rms(x_ref[...], gn).astype(BF16)
    h_ref[HALO + TM:, :] = (_rms(xnext_ref[...], gn) * keep_next).astype(BF16)
    rows = TM + 2 * HALO

    def up(c):
        return (_dot(h_ref[...], w_ref[:, c:c + FF_CHUNK]),
                _dot(h_ref[HALO:HALO + TM, :], w_ref[:, D_FF + c:D_FF + c + FF_CHUNK]))

    nxt = up(0)
    for c in range(0, D_FF, FF_CHUNK):
        sl = slice(c, c + FF_CHUNK)
        g, val = nxt
        if c + FF_CHUNK < D_FF:
            nxt = up(c + FF_CHUNK)
        conv = (cw_ref[0:1, sl] * pltpu.roll(g, 1, 0) + cw_ref[1:2, sl] * g
                + cw_ref[2:3, sl] * pltpu.roll(g, rows - 1, 0) + cb_ref[:, sl])
        act_ref[:, sl] = (_silu(conv[HALO:HALO + TM]) * val).astype(BF16)


def _ffn_up(x, norm_g, w_up, conv_w, conv_b):
    tiles_per_seq = SEQ // TM
    per_tile = TM // HALO
    n_halo = TOKENS // HALO
    return pl.pallas_call(
        functools.partial(_ffn_up_kernel, tiles_per_seq=tiles_per_seq),
        grid=(TOKENS // TM,),
        in_specs=[pl.BlockSpec((TM, D_MODEL), lambda i: (i, 0)),
                  pl.BlockSpec((HALO, D_MODEL), lambda i: (jnp.maximum(i * per_tile - 1, 0), 0)),
                  pl.BlockSpec((HALO, D_MODEL), lambda i: (jnp.minimum((i + 1) * per_tile, n_halo - 1), 0)),
                  pl.BlockSpec((1, D_MODEL), lambda i: (0, 0)),
                  pl.BlockSpec((D_MODEL, 2 * D_FF), lambda i: (0, 0)),
                  pl.BlockSpec((3, D_FF), lambda i: (0, 0)),
                  pl.BlockSpec((1, D_FF), lambda i: (0, 0))],
        out_specs=pl.BlockSpec((TM, D_FF), lambda i: (i, 0)),
        out_shape=jax.ShapeDtypeStruct((TOKENS, D_FF), BF16),
        scratch_shapes=[pltpu.VMEM((TM + 2 * HALO, D_MODEL), BF16)],
        compiler_params=_params("parallel"),
        name="ffn_up",
    )(x, x, x, norm_g.reshape(1, D_MODEL), w_up, conv_w, conv_b.reshape(1, D_FF))


def _conv_ffn(x, norm_g, w_up, conv_w, conv_b, w_down):
    act = _ffn_up(x, norm_g, w_up, conv_w, conv_b)
    return _out_proj(x, (act,), (w_down,), "ffn_down")


def _mla_prep_kernel(cq_ref, ckv_ref, kr_ref, cos_ref, sin_ref, qn_ref, kvn_ref, wuq_ref, wk_ref, wvt_ref,
                     ones_ref, gq_ref, gk_ref, q_out, k_out, vt_out):
    cos = cos_ref[...]
    sin = sin_ref[...]
    lane = lax.broadcasted_iota(jnp.int32, (1, LANES), 1)
    take_upper = lane < MLA_NOPE + MLA_ROPE // 2
    ones_row = (lax.broadcasted_iota(jnp.int32, (VT_ROWS, 1), 0) == MLA_V).astype(F32)
    half = MLA_ROPE // 2
    group = 2 * LANES

    def rope(x):
        rolled = jnp.where(take_upper, pltpu.roll(x, LANES - half, 1), pltpu.roll(x, half, 1))
        return x * cos + rolled * sin

    def slot_sumsq(x):
        x2 = (x * x).astype(BF16)
        return jnp.concatenate([_dot(x2[:, j:j + group], ones_ref[...]) for j in range(0, x.shape[1], group)],
                               axis=1)

    cqn = _rms(cq_ref[...].astype(F32), qn_ref[...]).astype(BF16)
    ckvn = _rms(ckv_ref[...].astype(F32), kvn_ref[...]).astype(BF16)
    scale = MLA_QK ** -0.5 * LOG2E
    qf = _dot(cqn, wuq_ref[...])
    q_inv = lax.rsqrt(slot_sumsq(qf) * (1.0 / MLA_QK) + EPS) * scale
    kf = _dot(ckvn, wk_ref[...])
    kr = kr_ref[...].astype(F32)
    kr_ss = jnp.sum(kr * kr, axis=-1, keepdims=True)
    k_inv = lax.rsqrt((slot_sumsq(kf) + kr_ss) * (1.0 / MLA_QK) + EPS)
    kr_rot = rope(kr * gk_ref[...])
    vt_all = _dot_nt(wvt_ref[...], ckvn)
    for h in range(MLA_HEADS):
        sl = slice(h * LANES, (h + 1) * LANES)
        q_out[0, h] = (rope(qf[:, sl] * gq_ref[...]) * q_inv[:, sl]).astype(BF16)
        k_out[0, h] = ((kf[:, sl] * gk_ref[...] + kr_rot) * k_inv[:, sl]).astype(BF16)
        vt_out[0, h, 0] = (vt_all[h * VT_ROWS:(h + 1) * VT_ROWS] + ones_row).astype(BF16)


def _mla_prep(cq, ckv, kr, cos, sin, q_norm, kv_norm, wuq_pad, wk_pad, wvt_pad, e_mat, gq_pad, gk_pad):
    assert TM == T_ATT_K
    tps = SEQ // TM
    tok = lambda n: pl.BlockSpec((TM, n), lambda i: (i, 0))
    full = lambda a: pl.BlockSpec(a.shape, lambda i: (0,) * a.ndim)
    head_major = pl.BlockSpec((1, MLA_HEADS, TM, LANES), lambda i: (i // tps, 0, i % tps, 0))
    head_major_t = pl.BlockSpec((1, MLA_HEADS, 1, VT_ROWS, TM), lambda i: (i // tps, 0, i % tps, 0, 0))
    args = (cq, ckv, kr, cos, sin, q_norm, kv_norm, wuq_pad, wk_pad, wvt_pad, e_mat, gq_pad, gk_pad)
    return pl.pallas_call(
        _mla_prep_kernel,
        grid=(TOKENS // TM,),
        in_specs=[tok(MLA_Q_RANK), tok(MLA_KV_RANK), tok(LANES), tok(LANES), tok(LANES)]
                 + [full(a) for a in args[5:]],
        out_specs=[head_major, head_major, head_major_t],
        out_shape=[jax.ShapeDtypeStruct((BATCH, MLA_HEADS, SEQ, LANES), BF16)] * 2
                  + [jax.ShapeDtypeStruct((BATCH, MLA_HEADS, SEQ // T_ATT_K, VT_ROWS, T_ATT_K), BF16)],
        compiler_params=_params("parallel"),
        name="mla_prep",
    )(*args)


def _attn_kernel(q_ref, k_ref, vt_ref, o_ref, m_ref, acc_ref, s_ref, bmax_ref, qt_ref):
    nkb = SEQ // T_ATT_K
    nqt = SEQ // T_ATT_Q
    lane = lax.broadcasted_iota(jnp.int32, (1, LANES), 1)
    acc_ref[...] = jnp.zeros(acc_ref.shape, F32)

    def load_q(qt):
        qoff = pl.multiple_of(qt * T_ATT_Q, T_ATT_Q)
        for hh in range(2):
            qt_ref[hh] = q_ref[0, hh, pl.ds(qoff, T_ATT_Q), :].T

    def scores(kb, slot):
        koff = pl.multiple_of(kb * T_ATT_K, T_ATT_K)
        for hh in range(2):
            st = _dot(k_ref[0, hh, pl.ds(koff, T_ATT_K), :], qt_ref[hh])
            s_ref[slot, hh] = st
            bmax_ref[slot, hh] = jnp.max(st, axis=0, keepdims=True)

    def consume(kb, slot):
        for hh in range(2):
            m_old = m_ref[hh]
            m_new = jnp.maximum(m_old, bmax_ref[slot, hh])
            p = jnp.exp2(s_ref[slot, hh] - m_new).astype(BF16)
            acc_ref[hh, 0:VT_ROWS, :] = (acc_ref[hh, 0:VT_ROWS, :] * jnp.exp2(m_old - m_new)
                                         + _dot(vt_ref[0, hh, kb], p))
            m_ref[hh] = m_new

    load_q(0)
    scores(0, 0)

    def tile(qt, carry):
        m_ref[...] = jnp.full(m_ref.shape, -jnp.inf, F32)
        acc_ref[:, 0:VT_ROWS, :] = jnp.zeros((2, VT_ROWS, T_ATT_Q), F32)

        for kb in range(nkb):
            if kb + 1 < nkb:
                scores(kb + 1, (kb + 1) % 2)
            else:
                load_q(jnp.minimum(qt + 1, nqt - 1))
                scores(0, 0)
            consume(kb, kb % 2)
        outs = []
        for hh in range(2):
            acc = acc_ref[hh].T
            outs.append(acc / acc[:, MLA_V:MLA_V + 1])
        o = jnp.where(lane < MLA_V, outs[0], pltpu.roll(outs[1], MLA_V, 1))
        o_ref[0, pl.ds(pl.multiple_of(qt * T_ATT_Q, T_ATT_Q), T_ATT_Q), :] = o.astype(BF16)
        return carry

    lax.fori_loop(0, nqt, tile, 0)


def _attention(q, k, vt):
    nkb = SEQ // T_ATT_K
    return pl.pallas_call(
        _attn_kernel,
        grid=(BATCH, MLA_HEADS // 2),
        in_specs=[pl.BlockSpec((1, 2, SEQ, LANES), lambda b, h: (b, h, 0, 0)),
                  pl.BlockSpec((1, 2, SEQ, LANES), lambda b, h: (b, h, 0, 0)),
                  pl.BlockSpec((1, 2, nkb, VT_ROWS, T_ATT_K), lambda b, h: (b, h, 0, 0, 0))],
        out_specs=pl.BlockSpec((1, SEQ, LANES), lambda b, h: (b, 0, h)),
        out_shape=jax.ShapeDtypeStruct((BATCH, SEQ, MLA_HEADS * MLA_V), BF16),
        scratch_shapes=[pltpu.VMEM((2, 1, T_ATT_Q), F32), pltpu.VMEM((2, LANES, T_ATT_Q), F32),
                        pltpu.VMEM((2, 2, T_ATT_K, T_ATT_Q), F32), pltpu.VMEM((2, 2, 1, T_ATT_Q), F32),
                        pltpu.VMEM((2, LANES, T_ATT_Q), BF16)],
        compiler_params=_params("parallel", "parallel"),
        name="mla_attention",
    )(q, k, vt)


def _ret_rope(x, cos, sin, lower_half):
    half = RET_DK // 2
    rolled = jnp.where(lower_half, pltpu.roll(x, LANES - half, 1), pltpu.roll(x, half, 1))
    return x * cos + rolled * sin


def _ret_common():
    lane = lax.broadcasted_iota(jnp.int32, (1, LANES), 1)
    rowi = lax.broadcasted_iota(jnp.int32, (RET_CHUNK, 1), 0)
    head0 = lane < RET_DK
    same_head = (rowi < RET_DK) == head0
    lower_half = (lane % RET_DK) < RET_DK // 2
    return lane, rowi, head0, same_head, lower_half


def _ret_fwd_kernel(q_ref, k_ref, v_ref, cos_ref, sin_ref, lgf_ref, lgb_ref, o_ref, qr_ref, kzb_ref, state_ref):
    @pl.when(pl.program_id(2) == 0)
    def _():
        state_ref[...] = jnp.zeros_like(state_ref)

    lane, rowi, head0, same_head, lower_half = _ret_common()
    C = RET_CHUNK
    row = rowi.astype(F32)
    col = lane.astype(F32)
    lgf = lgf_ref[0]
    lgb = lgb_ref[0]
    rel = row - col

    def decay(lf, lb):
        return jnp.exp(jnp.where(rel >= 0, lf * rel, -lb * rel))

    d0 = decay(lgf[:, 0:1], lgb[:, 0:1])
    d1 = decay(lgf[:, RET_DK:RET_DK + 1], lgb[:, RET_DK:RET_DK + 1])
    xi = jnp.exp(lgf * (row + 1.0))
    zeta = jnp.exp(lgf * (C - 1.0 - row))
    chunk_decay = jnp.exp(lgf * C)
    n = T_SCAN // C
    cos = cos_ref[...]
    sin = sin_ref[...]
    q = _ret_rope(q_ref[...].astype(F32), cos, sin, lower_half)
    k = _ret_rope(k_ref[...].astype(F32), cos, sin, lower_half) * (RET_DK ** -0.5)
    qb = q.astype(BF16)
    qr_ref[...] = qb
    zeta_b = jnp.exp(lgb * row)
    for c in range(n):
        kzb_ref[pl.ds(c * C, C), :] = (k[c * C:(c + 1) * C] * zeta_b).astype(BF16)
    q0 = jnp.where(head0, q, 0.0).astype(BF16)
    q1 = jnp.where(head0, 0.0, q).astype(BF16)
    kb = k.astype(BF16)
    ch = lambda a, c: a[c * C:(c + 1) * C]
    vs = [v_ref[pl.ds(c * C, C), :] for c in range(n)]
    s0 = [(_dot_nt(ch(q0, c), ch(kb, c)) * d0).astype(BF16) for c in range(n)]
    s1 = [(_dot_nt(ch(q1, c), ch(kb, c)) * d1).astype(BF16) for c in range(n)]
    intra = [jnp.where(head0, _dot(s0[c], vs[c]), _dot(s1[c], vs[c])) for c in range(n)]
    upd = [jnp.where(same_head, _dot_tn((ch(k, c) * zeta).astype(BF16), vs[c]), 0.0) for c in range(n)]
    state = state_ref[...]
    before = []
    for c in range(n):
        before.append(state.astype(BF16))
        state = state * chunk_decay + upd[c]
    state_ref[...] = state
    for c in range(n):
        o_ref[pl.ds(c * C, C), :] = intra[c] + _dot(ch(qb, c), before[c]) * xi


def _ret_bwd_kernel(qr_ref, kzb_ref, v_ref, g_ref, o1_ref, lgb_ref, gain_ref, o_ref, state_ref):
    @pl.when(pl.program_id(2) == 0)
    def _():
        state_ref[...] = jnp.zeros_like(state_ref)

    lane, rowi, head0, same_head, lower_half = _ret_common()
    C = RET_CHUNK
    row = rowi.astype(F32)
    lgb = lgb_ref[0]
    xi = jnp.exp(lgb * (C - row))
    chunk_decay = jnp.exp(lgb * C)
    n = T_SCAN // C
    upd = [jnp.where(same_head, _dot_tn(kzb_ref[pl.ds(c * C, C), :], v_ref[pl.ds(c * C, C), :]), 0.0)
           for c in range(n)]
    state = state_ref[...]
    before = [None] * n
    for c in reversed(range(n)):
        before[c] = state.astype(BF16)
        state = state * chunk_decay + upd[c]
    state_ref[...] = state
    for c in range(n):
        sl = pl.ds(c * C, C)
        o = o1_ref[sl, :] + _dot(qr_ref[sl, :], before[c]) * xi
        o2 = o * o
        ss0 = jnp.sum(jnp.where(head0, o2, 0.0), axis=-1, keepdims=True)
        ss1 = jnp.sum(jnp.where(head0, 0.0, o2), axis=-1, keepdims=True)
        ms = jnp.where(head0, ss0, ss1) * (1.0 / RET_DV)
        y = o * lax.rsqrt(ms + EPS) * gain_ref[...]
        o_ref[sl, :] = (_silu(g_ref[sl, :].astype(F32)) * y).astype(BF16)


def _retention(rq, rk, rv, rg, cos, sin, lgf_lane, lgb_lane, gain):
    nt = SEQ // T_SCAN
    pairs = RET_HEADS // 2
    fwd_tok = lambda b, p, t: (b * nt + t, p)
    fwd_tab = lambda b, p, t: (b * nt + t, 0)
    bwd_tok = lambda b, p, t: (b * nt + nt - 1 - t, p)
    bwd_tab = lambda b, p, t: (b * nt + nt - 1 - t, 0)
    per_pair = lambda b, p, t: (p, 0, 0)
    blk = (T_SCAN, LANES)
    width = RET_HEADS * RET_DV
    o1, qr, kzb = pl.pallas_call(
        _ret_fwd_kernel,
        grid=(BATCH, pairs, nt),
        in_specs=[pl.BlockSpec(blk, fwd_tok)] * 3 + [pl.BlockSpec(blk, fwd_tab)] * 2
                 + [pl.BlockSpec((1, 1, LANES), per_pair)] * 2,
        out_specs=[pl.BlockSpec(blk, fwd_tok)] * 3,
        out_shape=[jax.ShapeDtypeStruct((TOKENS, width), F32), jax.ShapeDtypeStruct((TOKENS, width), BF16),
                   jax.ShapeDtypeStruct((TOKENS, width), BF16)],
        scratch_shapes=[pltpu.VMEM((LANES, LANES), F32)],
        compiler_params=_params("parallel", "parallel", "arbitrary"),
        name="retention_fwd",
    )(rq, rk, rv, cos, sin, lgf_lane, lgb_lane)
    return pl.pallas_call(
        _ret_bwd_kernel,
        grid=(BATCH, pairs, nt),
        in_specs=[pl.BlockSpec(blk, bwd_tok)] * 5
                 + [pl.BlockSpec((1, 1, LANES), per_pair), pl.BlockSpec((1, LANES), lambda b, p, t: (0, p))],
        out_specs=pl.BlockSpec(blk, bwd_tok),
        out_shape=jax.ShapeDtypeStruct((TOKENS, width), BF16),
        scratch_shapes=[pltpu.VMEM((LANES, LANES), F32)],
        compiler_params=_params("parallel", "parallel", "arbitrary"),
        name="retention_bwd",
    )(qr, kzb, rv, rg, o1, lgb_lane, gain)


def _log_sigmoid(z):
    return jnp.minimum(z, 0.0) - jnp.log(1.0 + jnp.exp(-jnp.abs(z)))


def _gla_tile(q_ref, k_ref, v_ref, la_all, state_ref, reverse):
    C = GLA_CHUNK
    n = T_SCAN // C
    ri = lax.broadcasted_iota(jnp.int32, (C, C), 0)
    ci = lax.broadcasted_iota(jnp.int32, (C, C), 1)
    if reverse:
        tri, keep, mid, last = (ci >= ri).astype(BF16), ci > ri, C // 2 - 1, 0
    else:
        tri, keep, mid, last = (ri >= ci).astype(BF16), ri >= ci, C // 2, C - 1
    hi = la_all.astype(BF16)
    lo = (la_all - hi.astype(F32)).astype(BF16)
    hl = jnp.concatenate([hi, lo], axis=1)
    bs = []
    for c in range(n):
        r = _dot(tri, hl[c * C:(c + 1) * C])
        bs.append(r[:, :GLA_DK] + r[:, GLA_DK:])
    q = q_ref[...].astype(F32) * (GLA_DK ** -0.5)
    k = k_ref[...].astype(F32)
    qcs, kcs, qes, kds, decs = [], [], [], [], []
    for c in range(n):
        b = bs[c]
        qch = q[c * C:(c + 1) * C]
        kch = k[c * C:(c + 1) * C]
        b_mid = b[mid:mid + 1]
        b_last = b[last:last + 1]
        qcs.append((qch * jnp.exp(b - b_mid)).astype(BF16))
        kcs.append((kch * jnp.exp(b_mid - b)).astype(BF16))
        qes.append((qch * jnp.exp(b)).astype(BF16))
        kds.append((kch * jnp.exp(b_last - b)).astype(BF16))
        decs.append(jnp.exp(b_last))
    vs = [v_ref[pl.ds(c * C, C), :] for c in range(n)]
    attn = [jnp.where(keep, _dot_nt(qcs[c], kcs[c]), 0.0).astype(BF16) for c in range(n)]
    intra = [_dot(attn[c], vs[c]) for c in range(n)]
    upd = [_dot_tn(vs[c], kds[c]) for c in range(n)]
    state = state_ref[...]
    before = [None] * n
    for c in (reversed(range(n)) if reverse else range(n)):
        before[c] = state.astype(BF16)
        state = state * decs[c] + upd[c]
    state_ref[...] = state
    return [intra[c] + _dot_nt(qes[c], before[c]) for c in range(n)]


def _gla_gates(ga_ref, wg_ref, bg_ref):
    return _log_sigmoid(_dot(ga_ref[...], wg_ref[...]) + bg_ref[...]) * (1.0 / GLA_TAU)


def _gla_fwd_kernel(q_ref, k_ref, v_ref, ga_ref, wg_ref, bg_ref, o_ref, state_ref):
    @pl.when(pl.program_id(2) == 0)
    def _():
        state_ref[...] = jnp.zeros_like(state_ref)

    outs = _gla_tile(q_ref, k_ref, v_ref, _gla_gates(ga_ref, wg_ref, bg_ref), state_ref, reverse=False)
    for c, o in enumerate(outs):
        o_ref[pl.ds(c * GLA_CHUNK, GLA_CHUNK), :] = o


def _gla_bwd_kernel(q_ref, k_ref, v_ref, r_ref, o1_ref, ga_ref, wg_ref, bg_ref, gain_ref, o_ref, state_ref):
    @pl.when(pl.program_id(2) == 0)
    def _():
        state_ref[...] = jnp.zeros_like(state_ref)

    outs = _gla_tile(q_ref, k_ref, v_ref, _gla_gates(ga_ref, wg_ref, bg_ref), state_ref, reverse=True)
    for c, o in enumerate(outs):
        sl = pl.ds(c * GLA_CHUNK, GLA_CHUNK)
        y = _rms(o1_ref[sl, :] + o, gain_ref[...])
        o_ref[sl, :] = (_silu(r_ref[sl, :].astype(F32)) * y).astype(BF16)


def _gla(gq, gk, gv, gr, ga, wgf_pad, bgf, wgb_pad, bgb, gain):
    nt = SEQ // T_SCAN
    fwd_tok = lambda b, h, t: (b * nt + t, h)
    fwd_tab = lambda b, h, t: (b * nt + t, 0)
    bwd_tok = lambda b, h, t: (b * nt + nt - 1 - t, h)
    bwd_tab = lambda b, h, t: (b * nt + nt - 1 - t, 0)
    per_head = lambda b, h, t: (0, h)
    kblk = (T_SCAN, GLA_DK)
    vblk = (T_SCAN, GLA_DV)
    o1 = pl.pallas_call(
        _gla_fwd_kernel,
        grid=(BATCH, GLA_HEADS, nt),
        in_specs=[pl.BlockSpec(kblk, fwd_tok), pl.BlockSpec(kblk, fwd_tok), pl.BlockSpec(vblk, fwd_tok),
                  pl.BlockSpec((T_SCAN, LANES), fwd_tab),
                  pl.BlockSpec((LANES, GLA_DK), per_head), pl.BlockSpec((1, GLA_DK), per_head)],
        out_specs=pl.BlockSpec(vblk, fwd_tok),
        out_shape=jax.ShapeDtypeStruct((TOKENS, GLA_HEADS * GLA_DV), F32),
        scratch_shapes=[pltpu.VMEM((GLA_DV, GLA_DK), F32)],
        compiler_params=_params("parallel", "parallel", "arbitrary"),
        name="gla_fwd",
    )(gq, gk, gv, ga, wgf_pad, bgf)
    return pl.pallas_call(
        _gla_bwd_kernel,
        grid=(BATCH, GLA_HEADS, nt),
        in_specs=[pl.BlockSpec(kblk, bwd_tok), pl.BlockSpec(kblk, bwd_tok), pl.BlockSpec(vblk, bwd_tok),
                  pl.BlockSpec(vblk, bwd_tok), pl.BlockSpec(vblk, bwd_tok),
                  pl.BlockSpec((T_SCAN, LANES), bwd_tab),
                  pl.BlockSpec((LANES, GLA_DK), per_head), pl.BlockSpec((1, GLA_DK), per_head),
                  pl.BlockSpec((1, GLA_DV), per_head)],
        out_specs=pl.BlockSpec(vblk, bwd_tok),
        out_shape=jax.ShapeDtypeStruct((TOKENS, GLA_HEADS * GLA_DV), BF16),
        scratch_shapes=[pltpu.VMEM((GLA_DV, GLA_DK), F32)],
        compiler_params=_params("parallel", "parallel", "arbitrary"),
        name="gla_bwd",
    )(gq, gk, gv, gr, o1, ga, wgb_pad, bgb, gain)


def _even_weights(w_in, w_uq, w_ukv, q_head_norm, k_head_norm):
    kr0 = MLA_Q_RANK + MLA_KV_RANK
    w_in_r = jnp.concatenate(
        [w_in[:, :kr0], w_in[:, kr0 + MLA_ROPE:], jnp.zeros((D_MODEL, MLA_NOPE), w_in.dtype),
         w_in[:, kr0:kr0 + MLA_ROPE], jnp.zeros((D_MODEL, LANES - MLA_QK), w_in.dtype)], axis=1).astype(BF16)
    pad_q = jnp.zeros((MLA_Q_RANK, MLA_HEADS, LANES - MLA_QK), F32)
    wuq_pad = jnp.concatenate([w_uq.reshape(MLA_Q_RANK, MLA_HEADS, MLA_QK), pad_q], axis=2)
    wuq_pad = wuq_pad.reshape(MLA_Q_RANK, MLA_HEADS * LANES).astype(BF16)
    wkv = w_ukv.reshape(MLA_KV_RANK, MLA_HEADS, MLA_NOPE + MLA_V)
    pad_kv = jnp.zeros((MLA_KV_RANK, MLA_HEADS, LANES - MLA_NOPE), F32)
    wk_pad = jnp.concatenate([wkv[:, :, :MLA_NOPE], pad_kv], axis=2).reshape(MLA_KV_RANK, -1).astype(BF16)
    pad_v = jnp.zeros((MLA_KV_RANK, MLA_HEADS, VT_ROWS - MLA_V), F32)
    wv_pad = jnp.concatenate([wkv[:, :, MLA_NOPE:], pad_v], axis=2).reshape(MLA_KV_RANK, -1)
    wvt_pad = wv_pad.T.astype(BF16)
    ones_bd = np.kron(np.eye(2, dtype=np.float32), np.ones((LANES, LANES), np.float32))
    pad_g = jnp.zeros((LANES - MLA_QK,), F32)
    gq_pad = jnp.concatenate([q_head_norm, pad_g]).reshape(1, LANES)
    gk_pad = jnp.concatenate([k_head_norm, pad_g]).reshape(1, LANES)
    return w_in_r, wuq_pad, wk_pad, wvt_pad, jnp.asarray(ones_bd, BF16), gq_pad, gk_pad


def _odd_weights(w_in, w_gate_fwd, w_gate_bwd):
    w_in_r = jnp.concatenate(
        [w_in, jnp.zeros((D_MODEL, LANES - 2 * GLA_GATE_RANK), w_in.dtype)], axis=1).astype(BF16)
    n = GLA_HEADS * GLA_DK
    zf = jnp.zeros((LANES - GLA_GATE_RANK, n), F32)
    wgf_pad = jnp.concatenate([w_gate_fwd, zf], axis=0).astype(BF16)
    zb0 = jnp.zeros((GLA_GATE_RANK, n), F32)
    zb1 = jnp.zeros((LANES - 2 * GLA_GATE_RANK, n), F32)
    wgb_pad = jnp.concatenate([zb0, w_gate_bwd, zb1], axis=0).astype(BF16)
    return w_in_r, wgf_pad, wgb_pad


EVEN_SPLITS = ((0, 384), (384, 640), (640, 1152), (1152, 1664), (1664, 2176), (2176, 2688), (2688, 2816))
ODD_SPLITS = ((0, 512), (512, 1024), (1024, 2048), (2048, 3072), (3072, 3200))


def kernel(x, positions, mix_norm_even, w_in_even, mla_q_norm, mla_kv_norm, mla_w_uq, mla_w_ukv, mla_q_head_norm, mla_k_head_norm, ret_theta_fwd, ret_theta_bwd, ret_out_norm, w_out_even, mix_norm_odd, w_in_odd, gla_w_gate_fwd, gla_b_gate_fwd, gla_w_gate_bwd, gla_b_gate_bwd, gla_out_norm, w_out_odd, ffn_norm, ffn_w_up, ffn_conv_w, ffn_conv_b, ffn_w_down):
    x = x.reshape(TOKENS, D_MODEL)
    pos_col = positions.reshape(TOKENS, 1)
    ret_inv, ret_sgn, mla_inv, mla_sgn = _rope_lane_constants()
    ret_cos, ret_sin = _rope_tables(pos_col, ret_inv, ret_sgn)
    mla_cos, mla_sin = _rope_tables(pos_col, mla_inv, mla_sgn)

    for layer in range(DEPTH):
        i = layer // 2
        if layer % 2 == 0:
            w_in_r, wuq_pad, wk_pad, wvt_pad, e_mat, gq_pad, gk_pad = _even_weights(
                w_in_even[i], mla_w_uq[i], mla_w_ukv[i], mla_q_head_norm[i], mla_k_head_norm[i])
            cq, ckv, rq, rk, rv, rg, kr = _norm_proj(x, mix_norm_even[i], w_in_r, EVEN_SPLITS, "in_proj_even")
            q, k, v = _mla_prep(cq, ckv, kr, mla_cos, mla_sin,
                                mla_q_norm[i].reshape(1, -1), mla_kv_norm[i].reshape(1, -1),
                                wuq_pad, wk_pad, wvt_pad, e_mat, gq_pad, gk_pad)
            a = _attention(q, k, v).reshape(TOKENS, MLA_HEADS * MLA_V)
            lg_f = jnp.log1p(-jnp.exp2(-ret_theta_fwd[i].astype(F32)))
            lg_b = jnp.log1p(-jnp.exp2(-ret_theta_bwd[i].astype(F32)))
            lane_of = lambda lg: jnp.repeat(lg, RET_DK).reshape(RET_HEADS // 2, 1, LANES)
            r = _retention(rq, rk, rv, rg, ret_cos, ret_sin, lane_of(lg_f), lane_of(lg_b),
                           ret_out_norm[i].reshape(1, -1))
            w_out = w_out_even[i].astype(BF16)
            n_a = MLA_HEADS * MLA_V
            x = _out_proj(x, (a, r), (w_out[:n_a], w_out[n_a:]), "out_proj_even")
        else:
            w_in_r, wgf_pad, wgb_pad = _odd_weights(w_in_odd[i], gla_w_gate_fwd[i], gla_w_gate_bwd[i])
            gq, gk, gv, gr, ga = _norm_proj(x, mix_norm_odd[i], w_in_r, ODD_SPLITS, "in_proj_odd")
            g = _gla(gq, gk, gv, gr, ga, wgf_pad, gla_b_gate_fwd[i].reshape(1, -1),
                     wgb_pad, gla_b_gate_bwd[i].reshape(1, -1), gla_out_norm[i].reshape(1, -1))
            x = _out_proj(x, (g,), (w_out_odd[i].astype(BF16),), "out_proj_odd")
        x = _conv_ffn(x, ffn_norm[layer], ffn_w_up[layer].astype(BF16), ffn_conv_w[layer],
                      ffn_conv_b[layer], ffn_w_down[layer].astype(BF16))
    return x.reshape(BATCH, SEQ, D_MODEL)
```

```python
import functools

import numpy as np
import jax
import jax.numpy as jnp
from jax import lax
from jax.experimental import pallas as pl
from jax.experimental.pallas import tpu as pltpu

F32 = jnp.float32
BF16 = jnp.bfloat16

D_MODEL = 1024
BATCH = 4
SEQ = 8192
DEPTH = 4
TOKENS = BATCH * SEQ

MLA_HEADS = 8
MLA_Q_RANK = 384
MLA_KV_RANK = 256
MLA_NOPE = 64
MLA_ROPE = 32
MLA_V = 64
MLA_QK = MLA_NOPE + MLA_ROPE

RET_HEADS = 8
RET_DK = 64
RET_DV = 64
RET_CHUNK = 128

GLA_HEADS = 4
GLA_DK = 128
GLA_DV = 256
GLA_GATE_RANK = 16
GLA_TAU = 16.0
GLA_CHUNK = 64

D_FF = 2816
ROPE_THETA = 10000.0
EPS = 1e-6
LOG2E = 1.4426950408889634

LANES = 128
SUBLANES = 8
VMEM_LIMIT = 56 * 2**20

TM = 512
T_ATT_Q = 256
T_ATT_K = 512
VT_ROWS = 80
T_SCAN = 2048


def _params(*sem):
    return pltpu.CompilerParams(dimension_semantics=sem, vmem_limit_bytes=VMEM_LIMIT)


def _rms(x, g):
    return x * lax.rsqrt(jnp.mean(x * x, axis=-1, keepdims=True) + EPS) * g


def _dot(a, b):
    return jnp.dot(a, b, preferred_element_type=F32)


def _dot_nt(a, b):
    return lax.dot_general(a, b, (((1,), (1,)), ((), ())), preferred_element_type=F32)


def _dot_tn(a, b):
    return lax.dot_general(a, b, (((0,), (0,)), ((), ())), preferred_element_type=F32)


def _silu(x):
    return x * jax.nn.sigmoid(x)


def _rope_tables_kernel(pos_ref, inv_ref, sgn_ref, cos_ref, sin_ref):
    ang = pos_ref[...].astype(F32) * inv_ref[...]
    cos_ref[...] = jnp.cos(ang)
    sin_ref[...] = jnp.sin(ang) * sgn_ref[...]


def _rope_tables(pos_col, inv_lane, sgn_lane):
    return pl.pallas_call(
        _rope_tables_kernel,
        grid=(TOKENS // TM,),
        in_specs=[pl.BlockSpec((TM, 1), lambda i: (i, 0)),
                  pl.BlockSpec((1, LANES), lambda i: (0, 0)),
                  pl.BlockSpec((1, LANES), lambda i: (0, 0))],
        out_specs=[pl.BlockSpec((TM, LANES), lambda i: (i, 0))] * 2,
        out_shape=[jax.ShapeDtypeStruct((TOKENS, LANES), F32)] * 2,
        compiler_params=_params("parallel"),
        name="rope_tables",
    )(pos_col, inv_lane, sgn_lane)


def _rope_lane_constants():
    lane = np.arange(LANES)
    half = RET_DK // 2
    inv = ROPE_THETA ** (-np.arange(half, dtype=np.float32) / half)
    ret_inv = inv[lane % half].astype(np.float32)
    ret_sgn = np.where(lane % RET_DK < half, -1.0, 1.0).astype(np.float32)
    half = MLA_ROPE // 2
    inv = ROPE_THETA ** (-np.arange(half, dtype=np.float32) / half)
    in_rope = (lane >= MLA_NOPE) & (lane < MLA_QK)
    mla_inv = np.where(in_rope, inv[(lane - MLA_NOPE) % half], 0.0).astype(np.float32)
    mla_sgn = np.where(in_rope, np.where(lane < MLA_NOPE + half, -1.0, 1.0), 0.0).astype(np.float32)
    f = lambda a: jnp.asarray(a).reshape(1, LANES)
    return f(ret_inv), f(ret_sgn), f(mla_inv), f(mla_sgn)


def _norm_proj_kernel(x_ref, g_ref, w_ref, *out_refs, splits, chunk):
    h = _rms(x_ref[...], g_ref[...]).astype(BF16)
    for o_ref, (a, b) in zip(out_refs, splits):
        for c in range(a, b, chunk):
            e = min(c + chunk, b)
            o_ref[:, c - a:e - a] = _dot(h, w_ref[:, c:e]).astype(o_ref.dtype)


def _norm_proj(x, g, w, splits, name):
    n = w.shape[1]
    return pl.pallas_call(
        functools.partial(_norm_proj_kernel, splits=splits, chunk=512),
        grid=(TOKENS // TM,),
        in_specs=[pl.BlockSpec((TM, D_MODEL), lambda i: (i, 0)),
                  pl.BlockSpec((1, D_MODEL), lambda i: (0, 0)),
                  pl.BlockSpec((D_MODEL, n), lambda i: (0, 0))],
        out_specs=[pl.BlockSpec((TM, b - a), lambda i: (i, 0)) for a, b in splits],
        out_shape=[jax.ShapeDtypeStruct((TOKENS, b - a), BF16) for a, b in splits],
        compiler_params=_params("parallel"),
        name=name,
    )(x, g.reshape(1, D_MODEL), w)


def _out_proj_kernel(*refs, n_in):
    x_ref = refs[0]
    acts = refs[1:1 + n_in]
    ws = refs[1 + n_in:1 + 2 * n_in]
    o_ref = refs[1 + 2 * n_in]
    acc = x_ref[...]
    for a_ref, w_ref in zip(acts, ws):
        acc = acc + _dot(a_ref[...], w_ref[...])
    o_ref[...] = acc


def _out_proj(x, acts, ws, name):
    n_in = len(acts)
    return pl.pallas_call(
        functools.partial(_out_proj_kernel, n_in=n_in),
        grid=(TOKENS // TM,),
        in_specs=([pl.BlockSpec((TM, D_MODEL), lambda i: (i, 0))]
                  + [pl.BlockSpec((TM, a.shape[1]), lambda i: (i, 0)) for a in acts]
                  + [pl.BlockSpec(w.shape, lambda i: (0, 0)) for w in ws]),
        out_specs=pl.BlockSpec((TM, D_MODEL), lambda i: (i, 0)),
        out_shape=jax.ShapeDtypeStruct((TOKENS, D_MODEL), F32),
        compiler_params=_params("parallel"),
        name=name,
    )(x, *acts, *ws)


FF_CHUNK = 256
HALO = 16


def _ffn_up_kernel(*refs, n_mix, tiles_per_seq):
    x_ref, xprev_ref, xnext_ref = refs[0:3]
    mix_refs = [refs[3 + 3 * i:6 + 3 * i] for i in range(n_mix)]
    wo_refs = refs[3 + 3 * n_mix:3 + 4 * n_mix]
    g_ref, w_ref, cw_ref, cb_ref, x1_ref, act_ref, h_ref, gate_ref = refs[3 + 4 * n_mix:11 + 4 * n_mix]
    stage_refs = refs[11 + 4 * n_mix:]
    j = pl.program_id(0) % tiles_per_seq
    gn = g_ref[...]
    keep_prev = (j != 0).astype(F32)
    keep_next = (j != tiles_per_seq - 1).astype(F32)
    mixed = None
    for (m_main, m_prev, m_next), wo_ref, st_ref in zip(mix_refs, wo_refs, stage_refs):
        st_ref[0:HALO, :] = m_prev[...]
        st_ref[HALO:HALO + TM, :] = m_main[...]
        st_ref[HALO + TM:, :] = m_next[...]
        part = _dot(st_ref[...], wo_ref[...])
        mixed = part if mixed is None else mixed + part
    x1 = x_ref[...] + mixed[HALO:HALO + TM]
    x1_ref[...] = x1
    h_ref[0:HALO, :] = (_rms(xprev_ref[...] + mixed[0:HALO], gn) * keep_prev).astype(BF16)
    h_ref[HALO:HALO + TM, :] = _rms(x1, gn).astype(BF16)
    h_ref[HALO + TM:, :] = (_rms(xnext_ref[...] + mixed[HALO + TM:], gn) * keep_next).astype(BF16)
    rows = TM + 2 * HALO

    def up(c):
        return (_dot(h_ref[...], w_ref[:, c:c + FF_CHUNK]),
                _dot(h_ref[HALO:HALO + TM, :], w_ref[:, D_FF + c:D_FF + c + FF_CHUNK]))

    nxt = up(0)
    for c in range(0, D_FF, FF_CHUNK):
        sl = slice(c, c + FF_CHUNK)
        g, val = nxt
        if c + FF_CHUNK < D_FF:
            nxt = up(c + FF_CHUNK)
        stage = gate_ref.at[(c // FF_CHUNK) % 2]
        stage[...] = g
        conv = (cw_ref[0:1, sl] * stage[HALO - 1:HALO - 1 + TM, :] + cw_ref[1:2, sl] * stage[HALO:HALO + TM, :]
                + cw_ref[2:3, sl] * stage[HALO + 1:HALO + 1 + TM, :] + cb_ref[:, sl])
        act_ref[:, sl] = (_silu(conv) * val).astype(BF16)


def _mix_ffn_up(x, mixes, w_outs, norm_g, w_up, conv_w, conv_b):
    tiles_per_seq = SEQ // TM
    per_tile = TM // HALO
    n_halo = TOKENS // HALO
    rows = TM + 2 * HALO
    main = lambda i: (i, 0)
    prev = lambda i: (jnp.maximum(i * per_tile - 1, 0), 0)
    nxt = lambda i: (jnp.minimum((i + 1) * per_tile, n_halo - 1), 0)
    const = lambda i: (0, 0)
    with_halo = lambda n: [pl.BlockSpec((TM, n), main), pl.BlockSpec((HALO, n), prev), pl.BlockSpec((HALO, n), nxt)]
    in_specs = with_halo(D_MODEL)
    args = [x, x, x]
    for m in mixes:
        in_specs += with_halo(m.shape[1])
        args += [m, m, m]
    in_specs += [pl.BlockSpec(w.shape, const) for w in w_outs]
    args += list(w_outs)
    in_specs += [pl.BlockSpec((1, D_MODEL), const), pl.BlockSpec((D_MODEL, 2 * D_FF), const),
                 pl.BlockSpec((3, D_FF), const), pl.BlockSpec((1, D_FF), const)]
    args += [norm_g.reshape(1, D_MODEL), w_up, conv_w, conv_b.reshape(1, D_FF)]
    return pl.pallas_call(
        functools.partial(_ffn_up_kernel, n_mix=len(mixes), tiles_per_seq=tiles_per_seq),
        grid=(TOKENS // TM,),
        in_specs=in_specs,
        out_specs=[pl.BlockSpec((TM, D_MODEL), main), pl.BlockSpec((TM, D_FF), main)],
        out_shape=[jax.ShapeDtypeStruct((TOKENS, D_MODEL), F32), jax.ShapeDtypeStruct((TOKENS, D_FF), BF16)],
        scratch_shapes=[pltpu.VMEM((rows, D_MODEL), BF16), pltpu.VMEM((2, rows, FF_CHUNK), F32)]
                       + [pltpu.VMEM((rows, m.shape[1]), BF16) for m in mixes],
        compiler_params=_params("parallel"),
        name="mix_ffn_up",
    )(*args)


def _mix_conv_ffn(x, mixes, w_outs, norm_g, w_up, conv_w, conv_b, w_down):
    x1, act = _mix_ffn_up(x, mixes, w_outs, norm_g, w_up, conv_w, conv_b)
    return _out_proj(x1, (act,), (w_down,), "ffn_down")


def _mla_prep_kernel(cq_ref, ckv_ref, kr_ref, cos_ref, sin_ref, qn_ref, kvn_ref, wuq_ref, wk_ref, wvt_ref,
                     ones_ref, gq_ref, gk_ref, q_out, k_out, vt_out):
    cos = cos_ref[...]
    sin = sin_ref[...]
    lane = lax.broadcasted_iota(jnp.int32, (1, LANES), 1)
    take_upper = lane < MLA_NOPE + MLA_ROPE // 2
    ones_row = (lax.broadcasted_iota(jnp.int32, (VT_ROWS, 1), 0) == MLA_V).astype(F32)
    half = MLA_ROPE // 2
    group = 2 * LANES

    def rope(x):
        rolled = jnp.where(take_upper, pltpu.roll(x, LANES - half, 1), pltpu.roll(x, half, 1))
        return x * cos + rolled * sin

    def slot_sumsq(x):
        x2 = (x * x).astype(BF16)
        return jnp.concatenate([_dot(x2[:, j:j + group], ones_ref[...]) for j in range(0, x.shape[1], group)],
                               axis=1)

    cqn = _rms(cq_ref[...].astype(F32), qn_ref[...]).astype(BF16)
    ckvn = _rms(ckv_ref[...].astype(F32), kvn_ref[...]).astype(BF16)
    scale = MLA_QK ** -0.5 * LOG2E
    qf = _dot(cqn, wuq_ref[...])
    q_inv = lax.rsqrt(slot_sumsq(qf) * (1.0 / MLA_QK) + EPS) * scale
    kf = _dot(ckvn, wk_ref[...])
    kr = kr_ref[...].astype(F32)
    kr_ss = jnp.sum(kr * kr, axis=-1, keepdims=True)
    k_inv = lax.rsqrt((slot_sumsq(kf) + kr_ss) * (1.0 / MLA_QK) + EPS)
    kr_rot = rope(kr * gk_ref[...])
    vt_all = _dot_nt(wvt_ref[...], ckvn)
    for h in range(MLA_HEADS):
        sl = slice(h * LANES, (h + 1) * LANES)
        q_out[0, h] = (rope(qf[:, sl] * gq_ref[...]) * q_inv[:, sl]).astype(BF16)
        k_out[0, h] = ((kf[:, sl] * gk_ref[...] + kr_rot) * k_inv[:, sl]).astype(BF16)
        vt_out[0, h, 0] = (vt_all[h * VT_ROWS:(h + 1) * VT_ROWS] + ones_row).astype(BF16)


def _mla_prep(cq, ckv, kr, cos, sin, q_norm, kv_norm, wuq_pad, wk_pad, wvt_pad, e_mat, gq_pad, gk_pad):
    assert T_ATT_K % TM == 0
    tps = SEQ // TM
    per_kb = T_ATT_K // TM
    tok = lambda n: pl.BlockSpec((TM, n), lambda i: (i, 0))
    full = lambda a: pl.BlockSpec(a.shape, lambda i: (0,) * a.ndim)
    head_major = pl.BlockSpec((1, MLA_HEADS, TM, LANES), lambda i: (i // tps, 0, i % tps, 0))
    head_major_t = pl.BlockSpec((1, MLA_HEADS, 1, VT_ROWS, TM),
                                lambda i: (i // tps, 0, (i % tps) // per_kb, 0, (i % tps) % per_kb))
    args = (cq, ckv, kr, cos, sin, q_norm, kv_norm, wuq_pad, wk_pad, wvt_pad, e_mat, gq_pad, gk_pad)
    return pl.pallas_call(
        _mla_prep_kernel,
        grid=(TOKENS // TM,),
        in_specs=[tok(MLA_Q_RANK), tok(MLA_KV_RANK), tok(LANES), tok(LANES), tok(LANES)]
                 + [full(a) for a in args[5:]],
        out_specs=[head_major, head_major, head_major_t],
        out_shape=[jax.ShapeDtypeStruct((BATCH, MLA_HEADS, SEQ, LANES), BF16)] * 2
                  + [jax.ShapeDtypeStruct((BATCH, MLA_HEADS, SEQ // T_ATT_K, VT_ROWS, T_ATT_K), BF16)],
        compiler_params=_params("parallel"),
        name="mla_prep",
    )(*args)


def _attn_kernel(q_ref, k_ref, vt_ref, o_ref, m_ref, acc_ref, s_ref, bmax_ref, qt_ref):
    nkb = SEQ // T_ATT_K
    nqt = SEQ // T_ATT_Q
    lane = lax.broadcasted_iota(jnp.int32, (1, LANES), 1)
    acc_ref[...] = jnp.zeros(acc_ref.shape, F32)

    def load_q(qt):
        qoff = pl.multiple_of(qt * T_ATT_Q, T_ATT_Q)
        for hh in range(2):
            qt_ref[hh] = q_ref[0, hh, pl.ds(qoff, T_ATT_Q), :].T

    def scores(kb, slot):
        koff = pl.multiple_of(kb * T_ATT_K, T_ATT_K)
        for hh in range(2):
            st = _dot(k_ref[0, hh, pl.ds(koff, T_ATT_K), :], qt_ref[hh])
            s_ref[slot, hh] = st
            bmax_ref[slot, hh] = jnp.max(st, axis=0, keepdims=True)

    def consume(kb, slot, par):
        for hh in range(2):
            m_old = m_ref[hh]
            m_new = jnp.maximum(m_old, bmax_ref[slot, hh])
            p = jnp.exp2(s_ref[slot, hh] - m_new).astype(BF16)
            acc_ref[par, hh, 0:VT_ROWS, :] = (acc_ref[par, hh, 0:VT_ROWS, :] * jnp.exp2(m_old - m_new)
                                              + _dot(vt_ref[0, hh, kb], p))
            m_ref[hh] = m_new

    def finalize(par, qt):
        outs = []
        for hh in range(2):
            acc = acc_ref[par, hh].T
            outs.append(acc / acc[:, MLA_V:MLA_V + 1])
        o = jnp.where(lane < MLA_V, outs[0], pltpu.roll(outs[1], MLA_V, 1))
        o_ref[0, pl.ds(pl.multiple_of(qt * T_ATT_Q, T_ATT_Q), T_ATT_Q), :] = o.astype(BF16)

    load_q(0)
    scores(0, 0)

    def tile(qt, carry):
        par = qt % 2
        m_ref[...] = jnp.full(m_ref.shape, -jnp.inf, F32)
        acc_ref[par, :, 0:VT_ROWS, :] = jnp.zeros((2, VT_ROWS, T_ATT_Q), F32)

        for kb in range(nkb):
            if kb + 1 < nkb:
                scores(kb + 1, (kb + 1) % 2)
            else:
                load_q(jnp.minimum(qt + 1, nqt - 1))
                scores(0, 0)
            consume(kb, kb % 2, par)
            if kb == 0:
                finalize(1 - par, jnp.maximum(qt - 1, 0))
        return carry

    lax.fori_loop(0, nqt, tile, 0)
    finalize((nqt - 1) % 2, nqt - 1)


def _attention(q, k, vt):
    nkb = SEQ // T_ATT_K
    return pl.pallas_call(
        _attn_kernel,
        grid=(BATCH, MLA_HEADS // 2),
        in_specs=[pl.BlockSpec((1, 2, SEQ, LANES), lambda b, h: (b, h, 0, 0)),
                  pl.BlockSpec((1, 2, SEQ, LANES), lambda b, h: (b, h, 0, 0)),
                  pl.BlockSpec((1, 2, nkb, VT_ROWS, T_ATT_K), lambda b, h: (b, h, 0, 0, 0))],
        out_specs=pl.BlockSpec((1, SEQ, LANES), lambda b, h: (b, 0, h)),
        out_shape=jax.ShapeDtypeStruct((BATCH, SEQ, MLA_HEADS * MLA_V), BF16),
        scratch_shapes=[pltpu.VMEM((2, 1, T_ATT_Q), F32), pltpu.VMEM((2, 2, LANES, T_ATT_Q), F32),
                        pltpu.VMEM((2, 2, T_ATT_K, T_ATT_Q), F32), pltpu.VMEM((2, 2, 1, T_ATT_Q), F32),
                        pltpu.VMEM((2, LANES, T_ATT_Q), BF16)],
        compiler_params=_params("parallel", "parallel"),
        name="mla_attention",
    )(q, k, vt)


def _ret_rope(x, cos, sin, lower_half):
    half = RET_DK // 2
    rolled = jnp.where(lower_half, pltpu.roll(x, LANES - half, 1), pltpu.roll(x, half, 1))
    return x * cos + rolled * sin


def _ret_common():
    lane = lax.broadcasted_iota(jnp.int32, (1, LANES), 1)
    rowi = lax.broadcasted_iota(jnp.int32, (RET_CHUNK, 1), 0)
    head0 = lane < RET_DK
    same_head = (rowi < RET_DK) == head0
    lower_half = (lane % RET_DK) < RET_DK // 2
    return lane, rowi, head0, same_head, lower_half


def _ret_fwd_kernel(q_ref, k_ref, v_ref, cos_ref, sin_ref, lgf_ref, lgb_ref, o_ref, qr_ref, kzb_ref, state_ref):
    @pl.when(pl.program_id(2) == 0)
    def _():
        state_ref[...] = jnp.zeros_like(state_ref)

    lane, rowi, head0, same_head, lower_half = _ret_common()
    C = RET_CHUNK
    row = rowi.astype(F32)
    col = lane.astype(F32)
    lgf = lgf_ref[0]
    lgb = lgb_ref[0]
    rel = row - col

    def decay(lf, lb):
        return jnp.exp(jnp.where(rel >= 0, lf * rel, -lb * rel))

    d0 = decay(lgf[:, 0:1], lgb[:, 0:1])
    d1 = decay(lgf[:, RET_DK:RET_DK + 1], lgb[:, RET_DK:RET_DK + 1])
    xi = jnp.exp(lgf * (row + 1.0))
    zeta = jnp.exp(lgf * (C - 1.0 - row))
    chunk_decay = jnp.exp(lgf * C)
    n = T_SCAN // C
    cos = cos_ref[...]
    sin = sin_ref[...]
    q = _ret_rope(q_ref[...].astype(F32), cos, sin, lower_half)
    k = _ret_rope(k_ref[...].astype(F32), cos, sin, lower_half) * (RET_DK ** -0.5)
    qb = q.astype(BF16)
    qr_ref[...] = qb
    zeta_b = jnp.exp(lgb * row)
    for c in range(n):
        kzb_ref[pl.ds(c * C, C), :] = (k[c * C:(c + 1) * C] * zeta_b).astype(BF16)
    q0 = jnp.where(head0, q, 0.0).astype(BF16)
    q1 = jnp.where(head0, 0.0, q).astype(BF16)
    kb = k.astype(BF16)
    ch = lambda a, c: a[c * C:(c + 1) * C]
    vs = [v_ref[pl.ds(c * C, C), :] for c in range(n)]
    s0 = [(_dot_nt(ch(q0, c), ch(kb, c)) * d0).astype(BF16) for c in range(n)]
    s1 = [(_dot_nt(ch(q1, c), ch(kb, c)) * d1).astype(BF16) for c in range(n)]
    intra = [jnp.where(head0, _dot(s0[c], vs[c]), _dot(s1[c], vs[c])) for c in range(n)]
    upd = [jnp.where(same_head, _dot_tn((ch(k, c) * zeta).astype(BF16), vs[c]), 0.0) for c in range(n)]
    state = state_ref[...]
    before = []
    for c in range(n):
        before.append(state.astype(BF16))
        state = state * chunk_decay + upd[c]
    state_ref[...] = state
    for c in range(n):
        o_ref[pl.ds(c * C, C), :] = intra[c] + _dot(ch(qb, c), before[c]) * xi


def _ret_bwd_kernel(qr_ref, kzb_ref, v_ref, g_ref, o1_ref, lgb_ref, gain_ref, o_ref, state_ref):
    @pl.when(pl.program_id(2) == 0)
    def _():
        state_ref[...] = jnp.zeros_like(state_ref)

    lane, rowi, head0, same_head, lower_half = _ret_common()
    C = RET_CHUNK
    row = rowi.astype(F32)
    lgb = lgb_ref[0]
    xi = jnp.exp(lgb * (C - row))
    chunk_decay = jnp.exp(lgb * C)
    n = T_SCAN // C
    upd = [jnp.where(same_head, _dot_tn(kzb_ref[pl.ds(c * C, C), :], v_ref[pl.ds(c * C, C), :]), 0.0)
           for c in range(n)]
    state = state_ref[...]
    before = [None] * n
    for c in reversed(range(n)):
        before[c] = state.astype(BF16)
        state = state * chunk_decay + upd[c]
    state_ref[...] = state
    for c in range(n):
        sl = pl.ds(c * C, C)
        o = o1_ref[sl, :] + _dot(qr_ref[sl, :], before[c]) * xi
        o2 = o * o
        ss0 = jnp.sum(jnp.where(head0, o2, 0.0), axis=-1, keepdims=True)
        ss1 = jnp.sum(jnp.where(head0, 0.0, o2), axis=-1, keepdims=True)
        ms = jnp.where(head0, ss0, ss1) * (1.0 / RET_DV)
        y = o * lax.rsqrt(ms + EPS) * gain_ref[...]
        o_ref[sl, :] = (_silu(g_ref[sl, :].astype(F32)) * y).astype(BF16)


def _retention(rq, rk, rv, rg, cos, sin, lgf_lane, lgb_lane, gain):
    nt = SEQ // T_SCAN
    pairs = RET_HEADS // 2
    fwd_tok = lambda b, p, t: (b * nt + t, p)
    fwd_tab = lambda b, p, t: (b * nt + t, 0)
    bwd_tok = lambda b, p, t: (b * nt + nt - 1 - t, p)
    per_pair = lambda b, p, t: (p, 0, 0)
    blk = (T_SCAN, LANES)
    width = RET_HEADS * RET_DV
    o1, qr, kzb = pl.pallas_call(
        _ret_fwd_kernel,
        grid=(BATCH, pairs, nt),
        in_specs=[pl.BlockSpec(blk, fwd_tok)] * 3 + [pl.BlockSpec(blk, fwd_tab)] * 2
                 + [pl.BlockSpec((1, 1, LANES), per_pair)] * 2,
        out_specs=[pl.BlockSpec(blk, fwd_tok)] * 3,
        out_shape=[jax.ShapeDtypeStruct((TOKENS, width), F32), jax.ShapeDtypeStruct((TOKENS, width), BF16),
                   jax.ShapeDtypeStruct((TOKENS, width), BF16)],
        scratch_shapes=[pltpu.VMEM((LANES, LANES), F32)],
        compiler_params=_params("parallel", "parallel", "arbitrary"),
        name="retention_fwd",
    )(rq, rk, rv, cos, sin, lgf_lane, lgb_lane)
    return pl.pallas_call(
        _ret_bwd_kernel,
        grid=(BATCH, pairs, nt),
        in_specs=[pl.BlockSpec(blk, bwd_tok)] * 5
                 + [pl.BlockSpec((1, 1, LANES), per_pair), pl.BlockSpec((1, LANES), lambda b, p, t: (0, p))],
        out_specs=pl.BlockSpec(blk, bwd_tok),
        out_shape=jax.ShapeDtypeStruct((TOKENS, width), BF16),
        scratch_shapes=[pltpu.VMEM((LANES, LANES), F32)],
        compiler_params=_params("parallel", "parallel", "arbitrary"),
        name="retention_bwd",
    )(qr, kzb, rv, rg, o1, lgb_lane, gain)


def _log_sigmoid(z):
    return jnp.minimum(z, 0.0) - jnp.log(1.0 + jnp.exp(-jnp.abs(z)))


def _gla_tile(q_ref, k_ref, v_ref, la_all, state_ref, reverse):
    C = GLA_CHUNK
    n = T_SCAN // C
    ri = lax.broadcasted_iota(jnp.int32, (C, C), 0)
    ci = lax.broadcasted_iota(jnp.int32, (C, C), 1)
    if reverse:
        tri, keep, mid, last = (ci >= ri).astype(BF16), ci > ri, C // 2 - 1, 0
    else:
        tri, keep, mid, last = (ri >= ci).astype(BF16), ri >= ci, C // 2, C - 1
    hi = la_all.astype(BF16)
    lo = (la_all - hi.astype(F32)).astype(BF16)
    hl = jnp.concatenate([hi, lo], axis=1)
    bs = []
    for c in range(n):
        r = _dot(tri, hl[c * C:(c + 1) * C])
        bs.append(r[:, :GLA_DK] + r[:, GLA_DK:])
    q = q_ref[...].astype(F32) * (GLA_DK ** -0.5)
    k = k_ref[...].astype(F32)
    qcs, kcs, qes, kds, decs = [], [], [], [], []
    for c in range(n):
        b = bs[c]
        qch = q[c * C:(c + 1) * C]
        kch = k[c * C:(c + 1) * C]
        b_mid = b[mid:mid + 1]
        b_last = b[last:last + 1]
        qcs.append((qch * jnp.exp(b - b_mid)).astype(BF16))
        kcs.append((kch * jnp.exp(b_mid - b)).astype(BF16))
        qes.append((qch * jnp.exp(b)).astype(BF16))
        kds.append((kch * jnp.exp(b_last - b)).astype(BF16))
        decs.append(jnp.exp(b_last))
    vs = [v_ref[pl.ds(c * C, C), :] for c in range(n)]
    attn = [jnp.where(keep, _dot_nt(qcs[c], kcs[c]), 0.0).astype(BF16) for c in range(n)]
    intra = [_dot(attn[c], vs[c]) for c in range(n)]
    upd = [_dot_tn(vs[c], kds[c]) for c in range(n)]
    state = state_ref[...]
    before = [None] * n
    for c in (reversed(range(n)) if reverse else range(n)):
        before[c] = state.astype(BF16)
        state = state * decs[c] + upd[c]
    state_ref[...] = state
    return [intra[c] + _dot_nt(qes[c], before[c]) for c in range(n)]


def _gla_gates(ga_ref, wg_ref, bg_ref):
    return _log_sigmoid(_dot(ga_ref[...], wg_ref[...]) + bg_ref[...]) * (1.0 / GLA_TAU)


def _gla_fwd_kernel(q_ref, k_ref, v_ref, ga_ref, wg_ref, bg_ref, o_ref, state_ref):
    @pl.when(pl.program_id(2) == 0)
    def _():
        state_ref[...] = jnp.zeros_like(state_ref)

    outs = _gla_tile(q_ref, k_ref, v_ref, _gla_gates(ga_ref, wg_ref, bg_ref), state_ref, reverse=False)
    for c, o in enumerate(outs):
        o_ref[pl.ds(c * GLA_CHUNK, GLA_CHUNK), :] = o


def _gla_bwd_kernel(q_ref, k_ref, v_ref, r_ref, o1_ref, ga_ref, wg_ref, bg_ref, gain_ref, o_ref, state_ref):
    @pl.when(pl.program_id(2) == 0)
    def _():
        state_ref[...] = jnp.zeros_like(state_ref)

    outs = _gla_tile(q_ref, k_ref, v_ref, _gla_gates(ga_ref, wg_ref, bg_ref), state_ref, reverse=True)
    for c, o in enumerate(outs):
        sl = pl.ds(c * GLA_CHUNK, GLA_CHUNK)
        y = _rms(o1_ref[sl, :] + o, gain_ref[...])
        o_ref[sl, :] = (_silu(r_ref[sl, :].astype(F32)) * y).astype(BF16)


def _gla(gq, gk, gv, gr, ga, wgf_pad, bgf, wgb_pad, bgb, gain):
    nt = SEQ // T_SCAN
    fwd_tok = lambda b, h, t: (b * nt + t, h)
    fwd_tab = lambda b, h, t: (b * nt + t, 0)
    bwd_tok = lambda b, h, t: (b * nt + nt - 1 - t, h)
    bwd_tab = lambda b, h, t: (b * nt + nt - 1 - t, 0)
    per_head = lambda b, h, t: (0, h)
    kblk = (T_SCAN, GLA_DK)
    vblk = (T_SCAN, GLA_DV)
    o1 = pl.pallas_call(
        _gla_fwd_kernel,
        grid=(BATCH, GLA_HEADS, nt),
        in_specs=[pl.BlockSpec(kblk, fwd_tok), pl.BlockSpec(kblk, fwd_tok), pl.BlockSpec(vblk, fwd_tok),
                  pl.BlockSpec((T_SCAN, LANES), fwd_tab),
                  pl.BlockSpec((LANES, GLA_DK), per_head), pl.BlockSpec((1, GLA_DK), per_head)],
        out_specs=pl.BlockSpec(vblk, fwd_tok),
        out_shape=jax.ShapeDtypeStruct((TOKENS, GLA_HEADS * GLA_DV), F32),
        scratch_shapes=[pltpu.VMEM((GLA_DV, GLA_DK), F32)],
        compiler_params=_params("parallel", "parallel", "arbitrary"),
        name="gla_fwd",
    )(gq, gk, gv, ga, wgf_pad, bgf)
    return pl.pallas_call(
        _gla_bwd_kernel,
        grid=(BATCH, GLA_HEADS, nt),
        in_specs=[pl.BlockSpec(kblk, bwd_tok), pl.BlockSpec(kblk, bwd_tok), pl.BlockSpec(vblk, bwd_tok),
                  pl.BlockSpec(vblk, bwd_tok), pl.BlockSpec(vblk, bwd_tok),
                  pl.BlockSpec((T_SCAN, LANES), bwd_tab),
                  pl.BlockSpec((LANES, GLA_DK), per_head), pl.BlockSpec((1, GLA_DK), per_head),
                  pl.BlockSpec((1, GLA_DV), per_head)],
        out_specs=pl.BlockSpec(vblk, bwd_tok),
        out_shape=jax.ShapeDtypeStruct((TOKENS, GLA_HEADS * GLA_DV), BF16),
        scratch_shapes=[pltpu.VMEM((GLA_DV, GLA_DK), F32)],
        compiler_params=_params("parallel", "parallel", "arbitrary"),
        name="gla_bwd",
    )(gq, gk, gv, gr, o1, ga, wgb_pad, bgb, gain)


def _even_weights(w_in, w_uq, w_ukv, q_head_norm, k_head_norm):
    kr0 = MLA_Q_RANK + MLA_KV_RANK
    w_in_r = jnp.concatenate(
        [w_in[:, :kr0], w_in[:, kr0 + MLA_ROPE:], jnp.zeros((D_MODEL, MLA_NOPE), w_in.dtype),
         w_in[:, kr0:kr0 + MLA_ROPE], jnp.zeros((D_MODEL, LANES - MLA_QK), w_in.dtype)], axis=1).astype(BF16)
    pad_q = jnp.zeros((MLA_Q_RANK, MLA_HEADS, LANES - MLA_QK), F32)
    wuq_pad = jnp.concatenate([w_uq.reshape(MLA_Q_RANK, MLA_HEADS, MLA_QK), pad_q], axis=2)
    wuq_pad = wuq_pad.reshape(MLA_Q_RANK, MLA_HEADS * LANES).astype(BF16)
    wkv = w_ukv.reshape(MLA_KV_RANK, MLA_HEADS, MLA_NOPE + MLA_V)
    pad_kv = jnp.zeros((MLA_KV_RANK, MLA_HEADS, LANES - MLA_NOPE), F32)
    wk_pad = jnp.concatenate([wkv[:, :, :MLA_NOPE], pad_kv], axis=2).reshape(MLA_KV_RANK, -1).astype(BF16)
    pad_v = jnp.zeros((MLA_KV_RANK, MLA_HEADS, VT_ROWS - MLA_V), F32)
    wv_pad = jnp.concatenate([wkv[:, :, MLA_NOPE:], pad_v], axis=2).reshape(MLA_KV_RANK, -1)
    wvt_pad = wv_pad.T.astype(BF16)
    ones_bd = np.kron(np.eye(2, dtype=np.float32), np.ones((LANES, LANES), np.float32))
    pad_g = jnp.zeros((LANES - MLA_QK,), F32)
    gq_pad = jnp.concatenate([q_head_norm, pad_g]).reshape(1, LANES)
    gk_pad = jnp.concatenate([k_head_norm, pad_g]).reshape(1, LANES)
    return w_in_r, wuq_pad, wk_pad, wvt_pad, jnp.asarray(ones_bd, BF16), gq_pad, gk_pad


def _odd_weights(w_in, w_gate_fwd, w_gate_bwd):
    w_in_r = jnp.concatenate(
        [w_in, jnp.zeros((D_MODEL, LANES - 2 * GLA_GATE_RANK), w_in.dtype)], axis=1).astype(BF16)
    n = GLA_HEADS * GLA_DK
    zf = jnp.zeros((LANES - GLA_GATE_RANK, n), F32)
    wgf_pad = jnp.concatenate([w_gate_fwd, zf], axis=0).astype(BF16)
    zb0 = jnp.zeros((GLA_GATE_RANK, n), F32)
    zb1 = jnp.zeros((LANES - 2 * GLA_GATE_RANK, n), F32)
    wgb_pad = jnp.concatenate([zb0, w_gate_bwd, zb1], axis=0).astype(BF16)
    return w_in_r, wgf_pad, wgb_pad


EVEN_SPLITS = ((0, 384), (384, 640), (640, 1152), (1152, 1664), (1664, 2176), (2176, 2688), (2688, 2816))
ODD_SPLITS = ((0, 512), (512, 1024), (1024, 2048), (2048, 3072), (3072, 3200))


def kernel(x, positions, mix_norm_even, w_in_even, mla_q_norm, mla_kv_norm, mla_w_uq, mla_w_ukv, mla_q_head_norm, mla_k_head_norm, ret_theta_fwd, ret_theta_bwd, ret_out_norm, w_out_even, mix_norm_odd, w_in_odd, gla_w_gate_fwd, gla_b_gate_fwd, gla_w_gate_bwd, gla_b_gate_bwd, gla_out_norm, w_out_odd, ffn_norm, ffn_w_up, ffn_conv_w, ffn_conv_b, ffn_w_down):
    x = x.reshape(TOKENS, D_MODEL)
    pos_col = positions.reshape(TOKENS, 1)
    ret_inv, ret_sgn, mla_inv, mla_sgn = _rope_lane_constants()
    ret_cos, ret_sin = _rope_tables(pos_col, ret_inv, ret_sgn)
    mla_cos, mla_sin = _rope_tables(pos_col, mla_inv, mla_sgn)

    for layer in range(DEPTH):
        i = layer // 2
        if layer % 2 == 0:
            w_in_r, wuq_pad, wk_pad, wvt_pad, e_mat, gq_pad, gk_pad = _even_weights(
                w_in_even[i], mla_w_uq[i], mla_w_ukv[i], mla_q_head_norm[i], mla_k_head_norm[i])
            cq, ckv, rq, rk, rv, rg, kr = _norm_proj(x, mix_norm_even[i], w_in_r, EVEN_SPLITS, "in_proj_even")
            q, k, v = _mla_prep(cq, ckv, kr, mla_cos, mla_sin,
                                mla_q_norm[i].reshape(1, -1), mla_kv_norm[i].reshape(1, -1),
                                wuq_pad, wk_pad, wvt_pad, e_mat, gq_pad, gk_pad)
            a = _attention(q, k, v).reshape(TOKENS, MLA_HEADS * MLA_V)
            lg_f = jnp.log1p(-jnp.exp2(-ret_theta_fwd[i].astype(F32)))
            lg_b = jnp.log1p(-jnp.exp2(-ret_theta_bwd[i].astype(F32)))
            lane_of = lambda lg: jnp.repeat(lg, RET_DK).reshape(RET_HEADS // 2, 1, LANES)
            r = _retention(rq, rk, rv, rg, ret_cos, ret_sin, lane_of(lg_f), lane_of(lg_b),
                           ret_out_norm[i].reshape(1, -1))
            w_out = w_out_even[i].astype(BF16)
            n_a = MLA_HEADS * MLA_V
            mixes, w_outs = (a, r), (w_out[:n_a], w_out[n_a:])
        else:
            w_in_r, wgf_pad, wgb_pad = _odd_weights(w_in_odd[i], gla_w_gate_fwd[i], gla_w_gate_bwd[i])
            gq, gk, gv, gr, ga = _norm_proj(x, mix_norm_odd[i], w_in_r, ODD_SPLITS, "in_proj_odd")
            g = _gla(gq, gk, gv, gr, ga, wgf_pad, gla_b_gate_fwd[i].reshape(1, -1),
                     wgb_pad, gla_b_gate_bwd[i].reshape(1, -1), gla_out_norm[i].reshape(1, -1))
            mixes, w_outs = (g,), (w_out_odd[i].astype(BF16),)
        x = _mix_conv_ffn(x, mixes, w_outs, ffn_norm[layer], ffn_w_up[layer].astype(BF16), ffn_conv_w[layer],
                          ffn_conv_b[layer], ffn_w_down[layer].astype(BF16))
    return x.reshape(BATCH, SEQ, D_MODEL)
```

```python
import functools

import numpy as np
import jax
import jax.numpy as jnp
from jax import lax
from jax.experimental import pallas as pl
from jax.experimental.pallas import tpu as pltpu

F32 = jnp.float32
BF16 = jnp.bfloat16

D_MODEL = 1024
BATCH = 4
SEQ = 8192
DEPTH = 4
TOKENS = BATCH * SEQ

MLA_HEADS = 8
MLA_Q_RANK = 384
MLA_KV_RANK = 256
MLA_NOPE = 64
MLA_ROPE = 32
MLA_V = 64
MLA_QK = MLA_NOPE + MLA_ROPE

RET_HEADS = 8
RET_DK = 64
RET_DV = 64
RET_CHUNK = 128

GLA_HEADS = 4
GLA_DK = 128
GLA_DV = 256
GLA_GATE_RANK = 16
GLA_TAU = 16.0
GLA_CHUNK = 64

D_FF = 2816
ROPE_THETA = 10000.0
EPS = 1e-6
LOG2E = 1.4426950408889634

LANES = 128
SUBLANES = 8
VMEM_LIMIT = 56 * 2**20

TM = 512
T_ATT_Q = 512
T_ATT_K = 512
VT_ROWS = 80
T_SCAN = 4096


def _params(*sem):
    return pltpu.CompilerParams(dimension_semantics=sem, vmem_limit_bytes=VMEM_LIMIT)


def _rms(x, g):
    return x * lax.rsqrt(jnp.mean(x * x, axis=-1, keepdims=True) + EPS) * g


def _dot(a, b):
    return jnp.dot(a, b, preferred_element_type=F32)


def _dot_nt(a, b):
    return lax.dot_general(a, b, (((1,), (1,)), ((), ())), preferred_element_type=F32)


def _dot_tn(a, b):
    return lax.dot_general(a, b, (((0,), (0,)), ((), ())), preferred_element_type=F32)


def _silu(x):
    return x * jax.nn.sigmoid(x)


def _rope_tables_kernel(pos_ref, inv_ref, sgn_ref, cos_ref, sin_ref):
    ang = pos_ref[...].astype(F32) * inv_ref[...]
    cos_ref[...] = jnp.cos(ang)
    sin_ref[...] = jnp.sin(ang) * sgn_ref[...]


def _rope_tables(pos_col, inv_lane, sgn_lane):
    return pl.pallas_call(
        _rope_tables_kernel,
        grid=(TOKENS // TM,),
        in_specs=[pl.BlockSpec((TM, 1), lambda i: (i, 0)),
                  pl.BlockSpec((1, LANES), lambda i: (0, 0)),
                  pl.BlockSpec((1, LANES), lambda i: (0, 0))],
        out_specs=[pl.BlockSpec((TM, LANES), lambda i: (i, 0))] * 2,
        out_shape=[jax.ShapeDtypeStruct((TOKENS, LANES), F32)] * 2,
        compiler_params=_params("parallel"),
        name="rope_tables",
    )(pos_col, inv_lane, sgn_lane)


def _rope_lane_constants():
    lane = np.arange(LANES)
    half = RET_DK // 2
    inv = ROPE_THETA ** (-np.arange(half, dtype=np.float32) / half)
    ret_inv = inv[lane % half].astype(np.float32)
    ret_sgn = np.where(lane % RET_DK < half, -1.0, 1.0).astype(np.float32)
    half = MLA_ROPE // 2
    inv = ROPE_THETA ** (-np.arange(half, dtype=np.float32) / half)
    in_rope = (lane >= MLA_NOPE) & (lane < MLA_QK)
    mla_inv = np.where(in_rope, inv[(lane - MLA_NOPE) % half], 0.0).astype(np.float32)
    mla_sgn = np.where(in_rope, np.where(lane < MLA_NOPE + half, -1.0, 1.0), 0.0).astype(np.float32)
    f = lambda a: jnp.asarray(a).reshape(1, LANES)
    return f(ret_inv), f(ret_sgn), f(mla_inv), f(mla_sgn)


def _norm_proj_kernel(x_ref, g_ref, w_ref, *out_refs, splits, chunk):
    h = _rms(x_ref[...], g_ref[...]).astype(BF16)
    for o_ref, (a, b) in zip(out_refs, splits):
        for c in range(a, b, chunk):
            e = min(c + chunk, b)
            o_ref[:, c - a:e - a] = _dot(h, w_ref[:, c:e]).astype(o_ref.dtype)


def _norm_proj(x, g, w, splits, name):
    n = w.shape[1]
    return pl.pallas_call(
        functools.partial(_norm_proj_kernel, splits=splits, chunk=512),
        grid=(TOKENS // TM,),
        in_specs=[pl.BlockSpec((TM, D_MODEL), lambda i: (i, 0)),
                  pl.BlockSpec((1, D_MODEL), lambda i: (0, 0)),
                  pl.BlockSpec((D_MODEL, n), lambda i: (0, 0))],
        out_specs=[pl.BlockSpec((TM, b - a), lambda i: (i, 0)) for a, b in splits],
        out_shape=[jax.ShapeDtypeStruct((TOKENS, b - a), BF16) for a, b in splits],
        compiler_params=_params("parallel"),
        name=name,
    )(x, g.reshape(1, D_MODEL), w)


def _out_proj_kernel(*refs, n_in):
    x_ref = refs[0]
    acts = refs[1:1 + n_in]
    ws = refs[1 + n_in:1 + 2 * n_in]
    o_ref = refs[1 + 2 * n_in]
    acc = x_ref[...]
    for a_ref, w_ref in zip(acts, ws):
        acc = acc + _dot(a_ref[...], w_ref[...])
    o_ref[...] = acc


def _out_proj(x, acts, ws, name):
    n_in = len(acts)
    return pl.pallas_call(
        functools.partial(_out_proj_kernel, n_in=n_in),
        grid=(TOKENS // TM,),
        in_specs=([pl.BlockSpec((TM, D_MODEL), lambda i: (i, 0))]
                  + [pl.BlockSpec((TM, a.shape[1]), lambda i: (i, 0)) for a in acts]
                  + [pl.BlockSpec(w.shape, lambda i: (0, 0)) for w in ws]),
        out_specs=pl.BlockSpec((TM, D_MODEL), lambda i: (i, 0)),
        out_shape=jax.ShapeDtypeStruct((TOKENS, D_MODEL), F32),
        compiler_params=_params("parallel"),
        name=name,
    )(x, *acts, *ws)


FF_CHUNK = 256
HALO = 16


def _ffn_up_kernel(*refs, n_mix, tiles_per_seq):
    x_ref, xprev_ref, xnext_ref = refs[0:3]
    mix_refs = [refs[3 + 3 * i:6 + 3 * i] for i in range(n_mix)]
    wo_refs = refs[3 + 3 * n_mix:3 + 4 * n_mix]
    g_ref, w_ref, cw_ref, cb_ref, x1_ref, act_ref, h_ref, gate_ref = refs[3 + 4 * n_mix:11 + 4 * n_mix]
    stage_refs = refs[11 + 4 * n_mix:]
    j = pl.program_id(0) % tiles_per_seq
    gn = g_ref[...]
    keep_prev = (j != 0).astype(F32)
    keep_next = (j != tiles_per_seq - 1).astype(F32)
    mixed = None
    for (m_main, m_prev, m_next), wo_ref, st_ref in zip(mix_refs, wo_refs, stage_refs):
        st_ref[0:HALO, :] = m_prev[...]
        st_ref[HALO:HALO + TM, :] = m_main[...]
        st_ref[HALO + TM:, :] = m_next[...]
        part = _dot(st_ref[...], wo_ref[...])
        mixed = part if mixed is None else mixed + part
    x1 = x_ref[...] + mixed[HALO:HALO + TM]
    x1_ref[...] = x1
    h_ref[0:HALO, :] = (_rms(xprev_ref[...] + mixed[0:HALO], gn) * keep_prev).astype(BF16)
    h_ref[HALO:HALO + TM, :] = _rms(x1, gn).astype(BF16)
    h_ref[HALO + TM:, :] = (_rms(xnext_ref[...] + mixed[HALO + TM:], gn) * keep_next).astype(BF16)
    rows = TM + 2 * HALO

    def up(c):
        return (_dot(h_ref[...], w_ref[:, c:c + FF_CHUNK]),
                _dot(h_ref[HALO:HALO + TM, :], w_ref[:, D_FF + c:D_FF + c + FF_CHUNK]))

    nxt = up(0)
    for c in range(0, D_FF, FF_CHUNK):
        sl = slice(c, c + FF_CHUNK)
        g, val = nxt
        if c + FF_CHUNK < D_FF:
            nxt = up(c + FF_CHUNK)
        stage = gate_ref.at[(c // FF_CHUNK) % 2]
        stage[...] = g
        conv = (cw_ref[0:1, sl] * stage[HALO - 1:HALO - 1 + TM, :] + cw_ref[1:2, sl] * stage[HALO:HALO + TM, :]
                + cw_ref[2:3, sl] * stage[HALO + 1:HALO + 1 + TM, :] + cb_ref[:, sl])
        act_ref[:, sl] = (_silu(conv) * val).astype(BF16)


def _mix_ffn_up(x, mixes, w_outs, norm_g, w_up, conv_w, conv_b):
    tiles_per_seq = SEQ // TM
    per_tile = TM // HALO
    n_halo = TOKENS // HALO
    rows = TM + 2 * HALO
    main = lambda i: (i, 0)
    prev = lambda i: (jnp.maximum(i * per_tile - 1, 0), 0)
    nxt = lambda i: (jnp.minimum((i + 1) * per_tile, n_halo - 1), 0)
    const = lambda i: (0, 0)
    with_halo = lambda n: [pl.BlockSpec((TM, n), main), pl.BlockSpec((HALO, n), prev), pl.BlockSpec((HALO, n), nxt)]
    in_specs = with_halo(D_MODEL)
    args = [x, x, x]
    for m in mixes:
        in_specs += with_halo(m.shape[1])
        args += [m, m, m]
    in_specs += [pl.BlockSpec(w.shape, const) for w in w_outs]
    args += list(w_outs)
    in_specs += [pl.BlockSpec((1, D_MODEL), const), pl.BlockSpec((D_MODEL, 2 * D_FF), const),
                 pl.BlockSpec((3, D_FF), const), pl.BlockSpec((1, D_FF), const)]
    args += [norm_g.reshape(1, D_MODEL), w_up, conv_w, conv_b.reshape(1, D_FF)]
    return pl.pallas_call(
        functools.partial(_ffn_up_kernel, n_mix=len(mixes), tiles_per_seq=tiles_per_seq),
        grid=(TOKENS // TM,),
        in_specs=in_specs,
        out_specs=[pl.BlockSpec((TM, D_MODEL), main), pl.BlockSpec((TM, D_FF), main)],
        out_shape=[jax.ShapeDtypeStruct((TOKENS, D_MODEL), F32), jax.ShapeDtypeStruct((TOKENS, D_FF), BF16)],
        scratch_shapes=[pltpu.VMEM((rows, D_MODEL), BF16), pltpu.VMEM((2, rows, FF_CHUNK), F32)]
                       + [pltpu.VMEM((rows, m.shape[1]), BF16) for m in mixes],
        compiler_params=_params("parallel"),
        name="mix_ffn_up",
    )(*args)


def _mix_conv_ffn(x, mixes, w_outs, norm_g, w_up, conv_w, conv_b, w_down):
    x1, act = _mix_ffn_up(x, mixes, w_outs, norm_g, w_up, conv_w, conv_b)
    return _out_proj(x1, (act,), (w_down,), "ffn_down")


def _mla_prep_kernel(cq_ref, ckv_ref, kr_ref, cos_ref, sin_ref, qn_ref, kvn_ref, wuq_ref, wk_ref, wvt_ref,
                     ones_ref, gq_ref, gk_ref, q_out, k_out, vt_out):
    cos = cos_ref[...]
    sin = sin_ref[...]
    lane = lax.broadcasted_iota(jnp.int32, (1, LANES), 1)
    take_upper = lane < MLA_NOPE + MLA_ROPE // 2
    ones_row = (lax.broadcasted_iota(jnp.int32, (VT_ROWS, 1), 0) == MLA_V).astype(F32)
    half = MLA_ROPE // 2
    group = 2 * LANES

    def rope(x):
        rolled = jnp.where(take_upper, pltpu.roll(x, LANES - half, 1), pltpu.roll(x, half, 1))
        return x * cos + rolled * sin

    def slot_sumsq(x):
        x2 = (x * x).astype(BF16)
        return jnp.concatenate([_dot(x2[:, j:j + group], ones_ref[...]) for j in range(0, x.shape[1], group)],
                               axis=1)

    cqn = _rms(cq_ref[...].astype(F32), qn_ref[...]).astype(BF16)
    ckvn = _rms(ckv_ref[...].astype(F32), kvn_ref[...]).astype(BF16)
    scale = MLA_QK ** -0.5 * LOG2E
    qf = _dot(cqn, wuq_ref[...])
    q_inv = lax.rsqrt(slot_sumsq(qf) * (1.0 / MLA_QK) + EPS) * scale
    kf = _dot(ckvn, wk_ref[...])
    kr = kr_ref[...].astype(F32)
    kr_ss = jnp.sum(kr * kr, axis=-1, keepdims=True)
    k_inv = lax.rsqrt((slot_sumsq(kf) + kr_ss) * (1.0 / MLA_QK) + EPS)
    kr_rot = rope(kr * gk_ref[...])
    vt_all = _dot_nt(wvt_ref[...], ckvn)
    for h in range(MLA_HEADS):
        sl = slice(h * LANES, (h + 1) * LANES)
        q_out[0, h] = (rope(qf[:, sl] * gq_ref[...]) * q_inv[:, sl]).astype(BF16)
        k_out[0, h] = ((kf[:, sl] * gk_ref[...] + kr_rot) * k_inv[:, sl]).astype(BF16)
        vt_out[0, h, 0] = (vt_all[h * VT_ROWS:(h + 1) * VT_ROWS] + ones_row).astype(BF16)


def _mla_prep(cq, ckv, kr, cos, sin, q_norm, kv_norm, wuq_pad, wk_pad, wvt_pad, e_mat, gq_pad, gk_pad):
    assert T_ATT_K % TM == 0
    tps = SEQ // TM
    per_kb = T_ATT_K // TM
    tok = lambda n: pl.BlockSpec((TM, n), lambda i: (i, 0))
    full = lambda a: pl.BlockSpec(a.shape, lambda i: (0,) * a.ndim)
    head_major = pl.BlockSpec((1, MLA_HEADS, TM, LANES), lambda i: (i // tps, 0, i % tps, 0))
    head_major_t = pl.BlockSpec((1, MLA_HEADS, 1, VT_ROWS, TM),
                                lambda i: (i // tps, 0, (i % tps) // per_kb, 0, (i % tps) % per_kb))
    args = (cq, ckv, kr, cos, sin, q_norm, kv_norm, wuq_pad, wk_pad, wvt_pad, e_mat, gq_pad, gk_pad)
    return pl.pallas_call(
        _mla_prep_kernel,
        grid=(TOKENS // TM,),
        in_specs=[tok(MLA_Q_RANK), tok(MLA_KV_RANK), tok(LANES), tok(LANES), tok(LANES)]
                 + [full(a) for a in args[5:]],
        out_specs=[head_major, head_major, head_major_t],
        out_shape=[jax.ShapeDtypeStruct((BATCH, MLA_HEADS, SEQ, LANES), BF16)] * 2
                  + [jax.ShapeDtypeStruct((BATCH, MLA_HEADS, SEQ // T_ATT_K, VT_ROWS, T_ATT_K), BF16)],
        compiler_params=_params("parallel"),
        name="mla_prep",
    )(*args)


def _attn_kernel(q_ref, k_ref, vt_ref, o_ref, m_ref, acc_ref, s_ref, bmax_ref, qt_ref):
    nkb = SEQ // T_ATT_K
    nqt = SEQ // T_ATT_Q
    lane = lax.broadcasted_iota(jnp.int32, (1, LANES), 1)
    acc_ref[...] = jnp.zeros(acc_ref.shape, F32)

    def load_q(qt):
        qoff = pl.multiple_of(qt * T_ATT_Q, T_ATT_Q)
        for hh in range(2):
            qt_ref[hh] = q_ref[0, hh, pl.ds(qoff, T_ATT_Q), :].T

    def scores(kb, slot):
        koff = pl.multiple_of(kb * T_ATT_K, T_ATT_K)
        for hh in range(2):
            st = _dot(k_ref[0, hh, pl.ds(koff, T_ATT_K), :], qt_ref[hh])
            s_ref[slot, hh] = st
            bmax_ref[slot, hh] = jnp.max(st, axis=0, keepdims=True)

    def consume(kb, slot, par):
        for hh in range(2):
            m_old = m_ref[hh]
            m_new = jnp.maximum(m_old, bmax_ref[slot, hh])
            p = jnp.exp2(s_ref[slot, hh] - m_new).astype(BF16)
            acc_ref[par, hh, 0:VT_ROWS, :] = (acc_ref[par, hh, 0:VT_ROWS, :] * jnp.exp2(m_old - m_new)
                                              + _dot(vt_ref[0, hh, kb], p))
            m_ref[hh] = m_new

    def finalize(par, qt):
        outs = []
        for hh in range(2):
            acc = acc_ref[par, hh].T
            outs.append(acc / acc[:, MLA_V:MLA_V + 1])
        o = jnp.where(lane < MLA_V, outs[0], pltpu.roll(outs[1], MLA_V, 1))
        o_ref[0, pl.ds(pl.multiple_of(qt * T_ATT_Q, T_ATT_Q), T_ATT_Q), :] = o.astype(BF16)

    load_q(0)
    scores(0, 0)

    def tile(qt, carry):
        par = qt % 2
        m_ref[...] = jnp.full(m_ref.shape, -jnp.inf, F32)
        acc_ref[par, :, 0:VT_ROWS, :] = jnp.zeros((2, VT_ROWS, T_ATT_Q), F32)

        for kb in range(nkb):
            if kb + 1 < nkb:
                scores(kb + 1, (kb + 1) % 2)
            else:
                load_q(jnp.minimum(qt + 1, nqt - 1))
                scores(0, 0)
            consume(kb, kb % 2, par)
            if kb == 0:
                finalize(1 - par, jnp.maximum(qt - 1, 0))
        return carry

    lax.fori_loop(0, nqt, tile, 0)
    finalize((nqt - 1) % 2, nqt - 1)


def _attention(q, k, vt):
    nkb = SEQ // T_ATT_K
    return pl.pallas_call(
        _attn_kernel,
        grid=(BATCH, MLA_HEADS // 2),
        in_specs=[pl.BlockSpec((1, 2, SEQ, LANES), lambda b, h: (b, h, 0, 0)),
                  pl.BlockSpec((1, 2, SEQ, LANES), lambda b, h: (b, h, 0, 0)),
                  pl.BlockSpec((1, 2, nkb, VT_ROWS, T_ATT_K), lambda b, h: (b, h, 0, 0, 0))],
        out_specs=pl.BlockSpec((1, SEQ, LANES), lambda b, h: (b, 0, h)),
        out_shape=jax.ShapeDtypeStruct((BATCH, SEQ, MLA_HEADS * MLA_V), BF16),
        scratch_shapes=[pltpu.VMEM((2, 1, T_ATT_Q), F32), pltpu.VMEM((2, 2, LANES, T_ATT_Q), F32),
                        pltpu.VMEM((2, 2, T_ATT_K, T_ATT_Q), F32), pltpu.VMEM((2, 2, 1, T_ATT_Q), F32),
                        pltpu.VMEM((2, LANES, T_ATT_Q), BF16)],
        compiler_params=_params("parallel", "parallel"),
        name="mla_attention",
    )(q, k, vt)


def _ret_rope(x, cos, sin, lower_half):
    half = RET_DK // 2
    rolled = jnp.where(lower_half, pltpu.roll(x, LANES - half, 1), pltpu.roll(x, half, 1))
    return x * cos + rolled * sin


def _ret_common():
    lane = lax.broadcasted_iota(jnp.int32, (1, LANES), 1)
    rowi = lax.broadcasted_iota(jnp.int32, (RET_CHUNK, 1), 0)
    head0 = lane < RET_DK
    same_head = (rowi < RET_DK) == head0
    lower_half = (lane % RET_DK) < RET_DK // 2
    return lane, rowi, head0, same_head, lower_half


def _ret_fwd_kernel(q_ref, k_ref, v_ref, cos_ref, sin_ref, lgf_ref, lgb_ref, o_ref, qr_ref, kzb_ref, state_ref):
    @pl.when(pl.program_id(2) == 0)
    def _():
        state_ref[...] = jnp.zeros_like(state_ref)

    lane, rowi, head0, same_head, lower_half = _ret_common()
    C = RET_CHUNK
    row = rowi.astype(F32)
    col = lane.astype(F32)
    lgf = lgf_ref[0]
    lgb = lgb_ref[0]
    rel = row - col

    def decay(lf, lb):
        return jnp.exp(jnp.where(rel >= 0, lf * rel, -lb * rel))

    d0 = decay(lgf[:, 0:1], lgb[:, 0:1])
    d1 = decay(lgf[:, RET_DK:RET_DK + 1], lgb[:, RET_DK:RET_DK + 1])
    xi = jnp.exp(lgf * (row + 1.0))
    zeta = jnp.exp(lgf * (C - 1.0 - row))
    chunk_decay = jnp.exp(lgf * C)
    n = T_SCAN // C
    cos = cos_ref[...]
    sin = sin_ref[...]
    q = _ret_rope(q_ref[...].astype(F32), cos, sin, lower_half)
    k = _ret_rope(k_ref[...].astype(F32), cos, sin, lower_half) * (RET_DK ** -0.5)
    qb = q.astype(BF16)
    qr_ref[...] = qb
    zeta_b = jnp.exp(lgb * row)
    for c in range(n):
        kzb_ref[pl.ds(c * C, C), :] = (k[c * C:(c + 1) * C] * zeta_b).astype(BF16)
    q0 = jnp.where(head0, q, 0.0).astype(BF16)
    q1 = jnp.where(head0, 0.0, q).astype(BF16)
    kb = k.astype(BF16)
    ch = lambda a, c: a[c * C:(c + 1) * C]
    vs = [v_ref[pl.ds(c * C, C), :] for c in range(n)]
    s0 = [(_dot_nt(ch(q0, c), ch(kb, c)) * d0).astype(BF16) for c in range(n)]
    s1 = [(_dot_nt(ch(q1, c), ch(kb, c)) * d1).astype(BF16) for c in range(n)]
    intra = [jnp.where(head0, _dot(s0[c], vs[c]), _dot(s1[c], vs[c])) for c in range(n)]
    upd = [jnp.where(same_head, _dot_tn((ch(k, c) * zeta).astype(BF16), vs[c]), 0.0) for c in range(n)]
    state = state_ref[...]
    before = []
    for c in range(n):
        before.append(state.astype(BF16))
        state = state * chunk_decay + upd[c]
    state_ref[...] = state
    for c in range(n):
        o_ref[pl.ds(c * C, C), :] = intra[c] + _dot(ch(qb, c), before[c]) * xi


def _ret_bwd_kernel(qr_ref, kzb_ref, v_ref, g_ref, o1_ref, lgb_ref, gain_ref, o_ref, state_ref):
    @pl.when(pl.program_id(2) == 0)
    def _():
        state_ref[...] = jnp.zeros_like(state_ref)

    lane, rowi, head0, same_head, lower_half = _ret_common()
    C = RET_CHUNK
    row = rowi.astype(F32)
    lgb = lgb_ref[0]
    xi = jnp.exp(lgb * (C - row))
    chunk_decay = jnp.exp(lgb * C)
    n = T_SCAN // C
    upd = [jnp.where(same_head, _dot_tn(kzb_ref[pl.ds(c * C, C), :], v_ref[pl.ds(c * C, C), :]), 0.0)
           for c in range(n)]
    state = state_ref[...]
    before = [None] * n
    for c in reversed(range(n)):
        before[c] = state.astype(BF16)
        state = state * chunk_decay + upd[c]
    state_ref[...] = state
    for c in range(n):
        sl = pl.ds(c * C, C)
        o = o1_ref[sl, :] + _dot(qr_ref[sl, :], before[c]) * xi
        o2 = o * o
        ss0 = jnp.sum(jnp.where(head0, o2, 0.0), axis=-1, keepdims=True)
        ss1 = jnp.sum(jnp.where(head0, 0.0, o2), axis=-1, keepdims=True)
        ms = jnp.where(head0, ss0, ss1) * (1.0 / RET_DV)
        y = o * lax.rsqrt(ms + EPS) * gain_ref[...]
        o_ref[sl, :] = (_silu(g_ref[sl, :].astype(F32)) * y).astype(BF16)


def _retention(rq, rk, rv, rg, cos, sin, lgf_lane, lgb_lane, gain):
    nt = SEQ // T_SCAN
    pairs = RET_HEADS // 2
    fwd_tok = lambda b, p, t: (b * nt + t, p)
    fwd_tab = lambda b, p, t: (b * nt + t, 0)
    bwd_tok = lambda b, p, t: (b * nt + nt - 1 - t, p)
    per_pair = lambda b, p, t: (p, 0, 0)
    blk = (T_SCAN, LANES)
    width = RET_HEADS * RET_DV
    o1, qr, kzb = pl.pallas_call(
        _ret_fwd_kernel,
        grid=(BATCH, pairs, nt),
        in_specs=[pl.BlockSpec(blk, fwd_tok)] * 3 + [pl.BlockSpec(blk, fwd_tab)] * 2
                 + [pl.BlockSpec((1, 1, LANES), per_pair)] * 2,
        out_specs=[pl.BlockSpec(blk, fwd_tok)] * 3,
        out_shape=[jax.ShapeDtypeStruct((TOKENS, width), F32), jax.ShapeDtypeStruct((TOKENS, width), BF16),
                   jax.ShapeDtypeStruct((TOKENS, width), BF16)],
        scratch_shapes=[pltpu.VMEM((LANES, LANES), F32)],
        compiler_params=_params("parallel", "parallel", "arbitrary"),
        name="retention_fwd",
    )(rq, rk, rv, cos, sin, lgf_lane, lgb_lane)
    return pl.pallas_call(
        _ret_bwd_kernel,
        grid=(BATCH, pairs, nt),
        in_specs=[pl.BlockSpec(blk, bwd_tok)] * 5
                 + [pl.BlockSpec((1, 1, LANES), per_pair), pl.BlockSpec((1, LANES), lambda b, p, t: (0, p))],
        out_specs=pl.BlockSpec(blk, bwd_tok),
        out_shape=jax.ShapeDtypeStruct((TOKENS, width), BF16),
        scratch_shapes=[pltpu.VMEM((LANES, LANES), F32)],
        compiler_params=_params("parallel", "parallel", "arbitrary"),
        name="retention_bwd",
    )(qr, kzb, rv, rg, o1, lgb_lane, gain)


def _log_sigmoid(z):
    return jnp.minimum(z, 0.0) - jnp.log(1.0 + jnp.exp(-jnp.abs(z)))


def _gla_tile(q_ref, k_ref, v_ref, la_all, state_ref, reverse):
    C = GLA_CHUNK
    n = T_SCAN // C
    ri = lax.broadcasted_iota(jnp.int32, (C, C), 0)
    ci = lax.broadcasted_iota(jnp.int32, (C, C), 1)
    if reverse:
        tri, keep, mid, last = (ci >= ri).astype(BF16), ci > ri, C // 2 - 1, 0
    else:
        tri, keep, mid, last = (ri >= ci).astype(BF16), ri >= ci, C // 2, C - 1
    hi = la_all.astype(BF16)
    lo = (la_all - hi.astype(F32)).astype(BF16)
    hl = jnp.concatenate([hi, lo], axis=1)
    bs = []
    for c in range(n):
        r = _dot(tri, hl[c * C:(c + 1) * C])
        bs.append(r[:, :GLA_DK] + r[:, GLA_DK:])
    q = q_ref[...].astype(F32) * (GLA_DK ** -0.5)
    k = k_ref[...].astype(F32)
    qcs, kcs, qes, kds, decs = [], [], [], [], []
    for c in range(n):
        b = bs[c]
        qch = q[c * C:(c + 1) * C]
        kch = k[c * C:(c + 1) * C]
        b_mid = b[mid:mid + 1]
        b_last = b[last:last + 1]
        qcs.append((qch * jnp.exp2(b - b_mid)).astype(BF16))
        kcs.append((kch * jnp.exp2(b_mid - b)).astype(BF16))
        qes.append((qch * jnp.exp2(b)).astype(BF16))
        kds.append((kch * jnp.exp2(b_last - b)).astype(BF16))
        decs.append(jnp.exp2(b_last))
    vs = [v_ref[pl.ds(c * C, C), :] for c in range(n)]
    attn = [jnp.where(keep, _dot_nt(qcs[c], kcs[c]), 0.0).astype(BF16) for c in range(n)]
    intra = [_dot(attn[c], vs[c]) for c in range(n)]
    upd = [_dot_tn(vs[c], kds[c]) for c in range(n)]
    state = state_ref[...]
    before = [None] * n
    for c in (reversed(range(n)) if reverse else range(n)):
        before[c] = state.astype(BF16)
        state = state * decs[c] + upd[c]
    state_ref[...] = state
    return [intra[c] + _dot_nt(qes[c], before[c]) for c in range(n)]


def _gla_gates(ga_ref, wg_ref, bg_ref):
    return _log_sigmoid(_dot(ga_ref[...], wg_ref[...]) + bg_ref[...]) * (LOG2E / GLA_TAU)


def _gla_fwd_kernel(q_ref, k_ref, v_ref, ga_ref, wg_ref, bg_ref, o_ref, state_ref):
    @pl.when(pl.program_id(2) == 0)
    def _():
        state_ref[...] = jnp.zeros_like(state_ref)

    outs = _gla_tile(q_ref, k_ref, v_ref, _gla_gates(ga_ref, wg_ref, bg_ref), state_ref, reverse=False)
    for c, o in enumerate(outs):
        o_ref[pl.ds(c * GLA_CHUNK, GLA_CHUNK), :] = o


def _gla_bwd_kernel(q_ref, k_ref, v_ref, r_ref, o1_ref, ga_ref, wg_ref, bg_ref, gain_ref, o_ref, state_ref):
    @pl.when(pl.program_id(2) == 0)
    def _():
        state_ref[...] = jnp.zeros_like(state_ref)

    outs = _gla_tile(q_ref, k_ref, v_ref, _gla_gates(ga_ref, wg_ref, bg_ref), state_ref, reverse=True)
    for c, o in enumerate(outs):
        sl = pl.ds(c * GLA_CHUNK, GLA_CHUNK)
        y = _rms(o1_ref[sl, :] + o, gain_ref[...])
        o_ref[sl, :] = (_silu(r_ref[sl, :].astype(F32)) * y).astype(BF16)


def _gla(gq, gk, gv, gr, ga, wgf_pad, bgf, wgb_pad, bgb, gain):
    nt = SEQ // T_SCAN
    fwd_tok = lambda b, h, t: (b * nt + t, h)
    fwd_tab = lambda b, h, t: (b * nt + t, 0)
    bwd_tok = lambda b, h, t: (b * nt + nt - 1 - t, h)
    bwd_tab = lambda b, h, t: (b * nt + nt - 1 - t, 0)
    per_head = lambda b, h, t: (0, h)
    kblk = (T_SCAN, GLA_DK)
    vblk = (T_SCAN, GLA_DV)
    o1 = pl.pallas_call(
        _gla_fwd_kernel,
        grid=(BATCH, GLA_HEADS, nt),
        in_specs=[pl.BlockSpec(kblk, fwd_tok), pl.BlockSpec(kblk, fwd_tok), pl.BlockSpec(vblk, fwd_tok),
                  pl.BlockSpec((T_SCAN, LANES), fwd_tab),
                  pl.BlockSpec((LANES, GLA_DK), per_head), pl.BlockSpec((1, GLA_DK), per_head)],
        out_specs=pl.BlockSpec(vblk, fwd_tok),
        out_shape=jax.ShapeDtypeStruct((TOKENS, GLA_HEADS * GLA_DV), F32),
        scratch_shapes=[pltpu.VMEM((GLA_DV, GLA_DK), F32)],
        compiler_params=_params("parallel", "parallel", "arbitrary"),
        name="gla_fwd",
    )(gq, gk, gv, ga, wgf_pad, bgf)
    return pl.pallas_call(
        _gla_bwd_kernel,
        grid=(BATCH, GLA_HEADS, nt),
        in_specs=[pl.BlockSpec(kblk, bwd_tok), pl.BlockSpec(kblk, bwd_tok), pl.BlockSpec(vblk, bwd_tok),
                  pl.BlockSpec(vblk, bwd_tok), pl.BlockSpec(vblk, bwd_tok),
                  pl.BlockSpec((T_SCAN, LANES), bwd_tab),
                  pl.BlockSpec((LANES, GLA_DK), per_head), pl.BlockSpec((1, GLA_DK), per_head),
                  pl.BlockSpec((1, GLA_DV), per_head)],
        out_specs=pl.BlockSpec(vblk, bwd_tok),
        out_shape=jax.ShapeDtypeStruct((TOKENS, GLA_HEADS * GLA_DV), BF16),
        scratch_shapes=[pltpu.VMEM((GLA_DV, GLA_DK), F32)],
        compiler_params=_params("parallel", "parallel", "arbitrary"),
        name="gla_bwd",
    )(gq, gk, gv, gr, o1, ga, wgb_pad, bgb, gain)


def _even_weights(w_in, w_uq, w_ukv, q_head_norm, k_head_norm):
    kr0 = MLA_Q_RANK + MLA_KV_RANK
    w_in_r = jnp.concatenate(
        [w_in[:, :kr0], w_in[:, kr0 + MLA_ROPE:], jnp.zeros((D_MODEL, MLA_NOPE), w_in.dtype),
         w_in[:, kr0:kr0 + MLA_ROPE], jnp.zeros((D_MODEL, LANES - MLA_QK), w_in.dtype)], axis=1).astype(BF16)
    pad_q = jnp.zeros((MLA_Q_RANK, MLA_HEADS, LANES - MLA_QK), F32)
    wuq_pad = jnp.concatenate([w_uq.reshape(MLA_Q_RANK, MLA_HEADS, MLA_QK), pad_q], axis=2)
    wuq_pad = wuq_pad.reshape(MLA_Q_RANK, MLA_HEADS * LANES).astype(BF16)
    wkv = w_ukv.reshape(MLA_KV_RANK, MLA_HEADS, MLA_NOPE + MLA_V)
    pad_kv = jnp.zeros((MLA_KV_RANK, MLA_HEADS, LANES - MLA_NOPE), F32)
    wk_pad = jnp.concatenate([wkv[:, :, :MLA_NOPE], pad_kv], axis=2).reshape(MLA_KV_RANK, -1).astype(BF16)
    pad_v = jnp.zeros((MLA_KV_RANK, MLA_HEADS, VT_ROWS - MLA_V), F32)
    wv_pad = jnp.concatenate([wkv[:, :, MLA_NOPE:], pad_v], axis=2).reshape(MLA_KV_RANK, -1)
    wvt_pad = wv_pad.T.astype(BF16)
    ones_bd = np.kron(np.eye(2, dtype=np.float32), np.ones((LANES, LANES), np.float32))
    pad_g = jnp.zeros((LANES - MLA_QK,), F32)
    gq_pad = jnp.concatenate([q_head_norm, pad_g]).reshape(1, LANES)
    gk_pad = jnp.concatenate([k_head_norm, pad_g]).reshape(1, LANES)
    return w_in_r, wuq_pad, wk_pad, wvt_pad, jnp.asarray(ones_bd, BF16), gq_pad, gk_pad


def _odd_weights(w_in, w_gate_fwd, w_gate_bwd):
    w_in_r = jnp.concatenate(
        [w_in, jnp.zeros((D_MODEL, LANES - 2 * GLA_GATE_RANK), w_in.dtype)], axis=1).astype(BF16)
    n = GLA_HEADS * GLA_DK
    zf = jnp.zeros((LANES - GLA_GATE_RANK, n), F32)
    wgf_pad = jnp.concatenate([w_gate_fwd, zf], axis=0).astype(BF16)
    zb0 = jnp.zeros((GLA_GATE_RANK, n), F32)
    zb1 = jnp.zeros((LANES - 2 * GLA_GATE_RANK, n), F32)
    wgb_pad = jnp.concatenate([zb0, w_gate_bwd, zb1], axis=0).astype(BF16)
    return w_in_r, wgf_pad, wgb_pad


EVEN_SPLITS = ((0, 384), (384, 640), (640, 1152), (1152, 1664), (1664, 2176), (2176, 2688), (2688, 2816))
ODD_SPLITS = ((0, 512), (512, 1024), (1024, 2048), (2048, 3072), (3072, 3200))


def kernel(x, positions, mix_norm_even, w_in_even, mla_q_norm, mla_kv_norm, mla_w_uq, mla_w_ukv, mla_q_head_norm, mla_k_head_norm, ret_theta_fwd, ret_theta_bwd, ret_out_norm, w_out_even, mix_norm_odd, w_in_odd, gla_w_gate_fwd, gla_b_gate_fwd, gla_w_gate_bwd, gla_b_gate_bwd, gla_out_norm, w_out_odd, ffn_norm, ffn_w_up, ffn_conv_w, ffn_conv_b, ffn_w_down):
    x = x.reshape(TOKENS, D_MODEL)
    pos_col = positions.reshape(TOKENS, 1)
    ret_inv, ret_sgn, mla_inv, mla_sgn = _rope_lane_constants()
    ret_cos, ret_sin = _rope_tables(pos_col, ret_inv, ret_sgn)
    mla_cos, mla_sin = _rope_tables(pos_col, mla_inv, mla_sgn)

    for layer in range(DEPTH):
        i = layer // 2
        if layer % 2 == 0:
            w_in_r, wuq_pad, wk_pad, wvt_pad, e_mat, gq_pad, gk_pad = _even_weights(
                w_in_even[i], mla_w_uq[i], mla_w_ukv[i], mla_q_head_norm[i], mla_k_head_norm[i])
            cq, ckv, rq, rk, rv, rg, kr = _norm_proj(x, mix_norm_even[i], w_in_r, EVEN_SPLITS, "in_proj_even")
            q, k, v = _mla_prep(cq, ckv, kr, mla_cos, mla_sin,
                                mla_q_norm[i].reshape(1, -1), mla_kv_norm[i].reshape(1, -1),
                                wuq_pad, wk_pad, wvt_pad, e_mat, gq_pad, gk_pad)
            a = _attention(q, k, v).reshape(TOKENS, MLA_HEADS * MLA_V)
            lg_f = jnp.log1p(-jnp.exp2(-ret_theta_fwd[i].astype(F32)))
            lg_b = jnp.log1p(-jnp.exp2(-ret_theta_bwd[i].astype(F32)))
            lane_of = lambda lg: jnp.repeat(lg, RET_DK).reshape(RET_HEADS // 2, 1, LANES)
            r = _retention(rq, rk, rv, rg, ret_cos, ret_sin, lane_of(lg_f), lane_of(lg_b),
                           ret_out_norm[i].reshape(1, -1))
            w_out = w_out_even[i].astype(BF16)
            n_a = MLA_HEADS * MLA_V
            mixes, w_outs = (a, r), (w_out[:n_a], w_out[n_a:])
        else:
            w_in_r, wgf_pad, wgb_pad = _odd_weights(w_in_odd[i], gla_w_gate_fwd[i], gla_w_gate_bwd[i])
            gq, gk, gv, gr, ga = _norm_proj(x, mix_norm_odd[i], w_in_r, ODD_SPLITS, "in_proj_odd")
            g = _gla(gq, gk, gv, gr, ga, wgf_pad, gla_b_gate_fwd[i].reshape(1, -1),
                     wgb_pad, gla_b_gate_bwd[i].reshape(1, -1), gla_out_norm[i].reshape(1, -1))
            mixes, w_outs = (g,), (w_out_odd[i].astype(BF16),)
        x = _mix_conv_ffn(x, mixes, w_outs, ffn_norm[layer], ffn_w_up[layer].astype(BF16), ffn_conv_w[layer],
                          ffn_conv_b[layer], ffn_w_down[layer].astype(BF16))
    return x.reshape(BATCH, SEQ, D_MODEL)
```
